```python
import math
import jax, jax.numpy as jnp
from jax import lax
import numpy as np

D_MODEL = 1024
BATCH = 16
SEQ = 2048
DEPTH = 4

RET_HEADS = 4
RET_QK_DIM = 128
RET_V_DIM = 256
RET_QK_W = RET_HEADS * RET_QK_DIM
RET_V_W = RET_HEADS * RET_V_DIM
RET_CHUNK = 128
ROPE_BASE = 10000.0
POOL_WINDOWS = (2, 4, 8, 16)
POOL_GROUPS = 4
POOL_GROUP_DIM = D_MODEL // 8
POOL_W = POOL_GROUPS * POOL_GROUP_DIM
N_BRANCH = 2
IN_W = 2 * RET_QK_W + 2 * RET_V_W + POOL_W + N_BRANCH * D_MODEL
IN_SPLITS = (RET_QK_W, 2 * RET_QK_W, 2 * RET_QK_W + RET_V_W, 2 * RET_QK_W + 2 * RET_V_W,
             2 * RET_QK_W + 2 * RET_V_W + POOL_W, 2 * RET_QK_W + 2 * RET_V_W + POOL_W + D_MODEL)
N_EXPERTS = 16
N_GROUPS = 4
EXPERTS_PER_GROUP = N_EXPERTS // N_GROUPS
TOP_K = 2
D_EXPERT = D_MODEL // 2
MOE_BLOCK = 256
EPS = 1e-6

kernel_name = "hybrid_retention_pool_grouped_moe_adaln"


def rmsnorm(x, g):
    xf = x.astype(jnp.float32)
    y = xf * lax.rsqrt(jnp.mean(xf * xf, axis=-1, keepdims=True) + EPS)
    return y.astype(x.dtype) * g


def modulate(h, shift, scale):
    return h * (1.0 + scale[:, None, :]) + shift[:, None, :]


def rotary(t, pos):
    d = t.shape[-1]
    freqs = ROPE_BASE ** (-jnp.arange(0, d, 2, dtype=jnp.float32) / d)
    ang = pos[:, None] * freqs[None, :]
    cos = jnp.cos(ang).astype(t.dtype)
    sin = jnp.sin(ang).astype(t.dtype)
    t1, t2 = t[..., : d // 2], t[..., d // 2:]
    return jnp.concatenate([t1 * cos - t2 * sin, t1 * sin + t2 * cos], axis=-1)


def retention_chunkwise(q, k, v):
    b, h, s, dk = q.shape
    dv = v.shape[-1]
    c = RET_CHUNK
    n = s // c
    dt = q.dtype
    log_gamma = jnp.log(1.0 - 2.0 ** (-5.0 - jnp.arange(h, dtype=jnp.float32)))
    idx = jnp.arange(c, dtype=jnp.float32)
    rel = idx[:, None] - idx[None, :]
    causal = rel >= 0
    decay_in = jnp.where(causal[None], jnp.exp(log_gamma[:, None, None] * jnp.where(causal, rel, 0.0)[None]), 0.0)
    xi = jnp.exp(log_gamma[:, None] * (idx[None, :] + 1.0))
    zeta = jnp.exp(log_gamma[:, None] * (c - 1.0 - idx[None, :]))
    gamma_c = jnp.exp(log_gamma * c).astype(dt)
    qc = q.reshape(b, h, n, c, dk)
    kc = k.reshape(b, h, n, c, dk)
    vc = v.reshape(b, h, n, c, dv)
    scores = jnp.einsum('bhncd,bhnmd->bhncm', qc, kc) * decay_in[None, :, None].astype(dt)
    y_inner = jnp.einsum('bhncm,bhnme->bhnce', scores, vc)
    kv = jnp.einsum('bhnmd,bhnme->nbhde', kc * zeta[None, :, None, :, None].astype(dt), vc)

    def step(state, kv_n):
        return state * gamma_c[None, :, None, None] + kv_n, state

    _, prev = lax.scan(step, jnp.zeros((b, h, dk, dv), dt), kv)
    y_cross = jnp.einsum('bhncd,nbhde->bhnce', qc * xi[None, :, None, :, None].astype(dt), prev)
    return (y_inner + y_cross).reshape(b, h, s, dv)


def head_norm(y):
    yf = y.astype(jnp.float32)
    return (yf * lax.rsqrt(jnp.mean(yf * yf, axis=-1, keepdims=True) + EPS)).astype(y.dtype)


def multiscale_pool(u, pool_w, pool_scale):
    b, s, _ = u.shape
    ug = u.reshape(b, s, POOL_GROUPS, POOL_GROUP_DIM)
    cs = jnp.cumsum(ug.astype(jnp.float32), axis=1)
    cs = jnp.pad(cs, ((0, 0), (1, 0), (0, 0), (0, 0)))
    t = jnp.arange(s)
    win = jnp.array(POOL_WINDOWS, dtype=jnp.int32)
    lo = jnp.maximum(t[:, None] + 1 - win[None, :], 0)
    gidx = jnp.arange(POOL_GROUPS)[None, :]
    window_sum = cs[:, 1:] - cs[:, lo, gidx]
    count = jnp.minimum(t[:, None] + 1, win[None, :]).astype(jnp.float32)
    mix = (window_sum / count[None, :, :, None] - ug.astype(jnp.float32)).astype(u.dtype)
    y = jnp.einsum('bsgc,gcd->bsgd', mix, pool_w).reshape(b, s, POOL_W)
    return y * pool_scale


def mixer(h, w_in, w_ret_o, pool_w, pool_scale, w_pool_o, w_out):
    b, s, _ = h.shape
    proj = h @ w_in
    q, k, v, g, u, a_ret, a_pool = jnp.split(proj, IN_SPLITS, axis=-1)

    def heads(t, d):
        return t.reshape(b, s, RET_HEADS, d).transpose(0, 2, 1, 3)

    pos = jnp.arange(s, dtype=jnp.float32)
    qh = rotary(heads(q, RET_QK_DIM), pos)
    kh = rotary(heads(k, RET_QK_DIM), pos) * (RET_QK_DIM ** -0.5)
    y = retention_chunkwise(qh, kh, heads(v, RET_V_DIM))
    y = head_norm(y).transpose(0, 2, 1, 3).reshape(b, s, RET_V_W)
    y_ret = (jax.nn.silu(g) * y) @ w_ret_o
    y_pool = multiscale_pool(u, pool_w, pool_scale) @ w_pool_o
    merged = jax.nn.sigmoid(a_ret) * y_ret + jax.nn.sigmoid(a_pool) * y_pool
    return merged @ w_out


def route(hf, w_router, b_router):
    t = hf.shape[0]
    logits = (hf @ w_router).astype(jnp.float32) + b_router.astype(jnp.float32)
    probs = jax.nn.softmax(logits, axis=-1)
    grouped = probs.reshape(t, N_GROUPS, EXPERTS_PER_GROUP)
    group_score = lax.top_k(grouped, TOP_K)[0].sum(-1)
    sel_group = jnp.argmax(group_score, axis=-1)
    in_group = jnp.take_along_axis(grouped, sel_group[:, None, None], axis=1)[:, 0]
    top_p, top_local = lax.top_k(in_group, TOP_K)
    expert_idx = sel_group[:, None] * EXPERTS_PER_GROUP + top_local
    weights = top_p / jnp.sum(top_p, axis=-1, keepdims=True)
    return expert_idx, weights


def moe(h, w_router, b_router, w_gate, w_up, w_down):
    b, s, d = h.shape
    t = b * s
    hf = h.reshape(t, d)
    expert_idx, weights = route(hf, w_router, b_router)
    a = t * TOP_K
    e_flat = expert_idx.reshape(a)
    tok_flat = jnp.arange(a, dtype=jnp.int32) // TOP_K
    w_flat = weights.reshape(a)
    order = jnp.argsort(e_flat)
    e_sorted = e_flat[order]
    counts = jnp.bincount(e_flat, length=N_EXPERTS)
    padded = ((counts + MOE_BLOCK - 1) // MOE_BLOCK) * MOE_BLOCK
    start = jnp.cumsum(counts) - counts
    pad_end = jnp.cumsum(padded)
    pad_start = pad_end - padded
    dest = pad_start[e_sorted] + (jnp.arange(a) - start[e_sorted])
    n_slots = a + N_EXPERTS * MOE_BLOCK
    n_blocks = n_slots // MOE_BLOCK
    slot_tok = jnp.zeros((n_slots,), jnp.int32).at[dest].set(tok_flat[order])
    slot_w = jnp.zeros((n_slots,), jnp.float32).at[dest].set(w_flat[order])
    block_e = jnp.minimum(jnp.searchsorted(pad_end, jnp.arange(n_blocks) * MOE_BLOCK, side='right'), N_EXPERTS - 1)
    xb = hf[slot_tok].reshape(n_blocks, MOE_BLOCK, d)

    def expert_block(args):
        xblk, e = args
        return (jax.nn.silu(xblk @ w_gate[e]) * (xblk @ w_up[e])) @ w_down[e]

    yb = lax.map(expert_block, (xb, block_e))
    y_slots = yb.reshape(n_slots, d) * slot_w[:, None].astype(yb.dtype)
    return jax.ops.segment_sum(y_slots, slot_tok, num_segments=t).reshape(b, s, d)


def setup_inputs(seed: int = 0) -> dict:
    key = jax.random.key(seed)
    ks = jax.random.split(key, 20)
    f32 = jnp.float32
    nrm = lambda k, shape, sc: jax.random.normal(k, shape, f32) * sc
    return {
        "x": nrm(ks[0], (BATCH, SEQ, D_MODEL), 1.0),
        "c": nrm(ks[1], (BATCH, D_MODEL), 1.0),
        "w_ada": nrm(ks[2], (DEPTH, D_MODEL, 6 * D_MODEL), 0.5 * D_MODEL ** -0.5),
        "b_ada": nrm(ks[3], (DEPTH, 6 * D_MODEL), 0.02),
        "norm1_g": 1.0 + nrm(ks[4], (DEPTH, D_MODEL), 0.02),
        "norm2_g": 1.0 + nrm(ks[5], (DEPTH, D_MODEL), 0.02),
        "w_in": nrm(ks[6], (DEPTH, D_MODEL, IN_W), D_MODEL ** -0.5),
        "w_ret_o": nrm(ks[7], (DEPTH, RET_V_W, D_MODEL), RET_V_W ** -0.5),
        "pool_w": nrm(ks[8], (DEPTH, POOL_GROUPS, POOL_GROUP_DIM, POOL_GROUP_DIM), POOL_GROUP_DIM ** -0.5),
        "pool_scale": 1.0 + nrm(ks[9], (DEPTH, POOL_W), 0.1),
        "w_pool_o": nrm(ks[10], (DEPTH, POOL_W, D_MODEL), POOL_W ** -0.5),
        "w_out": nrm(ks[11], (DEPTH, D_MODEL, D_MODEL), D_MODEL ** -0.5),
        "w_router": nrm(ks[12], (D_MODEL, N_EXPERTS), D_MODEL ** -0.5),
        "b_router": nrm(ks[13], (N_EXPERTS,), 0.01),
        "w_e_gate": nrm(ks[14], (DEPTH, N_EXPERTS, D_MODEL, D_EXPERT), D_MODEL ** -0.5),
        "w_e_up": nrm(ks[15], (DEPTH, N_EXPERTS, D_MODEL, D_EXPERT), D_MODEL ** -0.5),
        "w_e_down": nrm(ks[16], (DEPTH, N_EXPERTS, D_EXPERT, D_MODEL), D_EXPERT ** -0.5),
        "final_g": 1.0 + nrm(ks[17], (D_MODEL,), 0.02),
    }


def reference(x, c, w_ada, b_ada, norm1_g, norm2_g, w_in, w_ret_o, pool_w, pool_scale, w_pool_o, w_out,
              w_router, b_router, w_e_gate, w_e_up, w_e_down, final_g):
    cond = jax.nn.silu(c)
    for l in range(DEPTH):
        mod = cond @ w_ada[l] + b_ada[l]
        sh1, sc1, g1, sh2, sc2, g2 = jnp.split(mod, 6, axis=-1)
        h = modulate(rmsnorm(x, norm1_g[l]), sh1, sc1)
        x = x + g1[:, None, :] * mixer(h, w_in[l], w_ret_o[l], pool_w[l], pool_scale[l], w_pool_o[l], w_out[l])
        h = modulate(rmsnorm(x, norm2_g[l]), sh2, sc2)
        x = x + g2[:, None, :] * moe(h, w_router, b_router, w_e_gate[l], w_e_up[l], w_e_down[l])
    return rmsnorm(x, final_g)
```

```python
import functools

import numpy as np
import jax
import jax.numpy as jnp
from jax import lax
from jax.experimental import pallas as pl
from jax.experimental.pallas import tpu as pltpu

D_MODEL = 1024
BATCH = 16
SEQ = 2048
DEPTH = 4
N_TOK = BATCH * SEQ

RET_HEADS = 4
RET_QK_DIM = 128
RET_V_DIM = 256
RET_QK_W = RET_HEADS * RET_QK_DIM
RET_V_W = RET_HEADS * RET_V_DIM
ROPE_BASE = 10000.0
POOL_WINDOWS = (2, 4, 8, 16)
POOL_GROUPS = 4
POOL_GROUP_DIM = 128
POOL_W = POOL_GROUPS * POOL_GROUP_DIM
POOL_HALO = 16
IN_W = 2 * RET_QK_W + 2 * RET_V_W + POOL_W + 2 * D_MODEL
OFF_Q = 0
OFF_K = OFF_Q + RET_QK_W
OFF_V = OFF_K + RET_QK_W
OFF_G = OFF_V + RET_V_W
OFF_U = OFF_G + RET_V_W
OFF_AR = OFF_U + POOL_W
OFF_AP = OFF_AR + D_MODEL

N_EXPERTS = 16
N_GROUPS = 4
EXPERTS_PER_GROUP = 4
TOP_K = 2
D_EXPERT = D_MODEL // 2
EPS = 1e-6

PROJ_TM = 512
PROJ_TN = 512
MIX_TS = 512
RET_C = 256
DISP_TD = 1024
EXP_BLK = 256
N_SLOTS = N_TOK * TOP_K + N_EXPERTS * EXP_BLK
N_BLOCKS = N_SLOTS // EXP_BLK
COMB_TC = DISP_TD
VMEM_LIMIT = 56 * 1024 * 1024

F32 = jnp.float32
BF16 = jnp.bfloat16


def _const_spec(shape):
    zeros = (0,) * len(shape)
    return pl.BlockSpec(shape, lambda *_: zeros, pipeline_mode=pl.Buffered(1))


def _rms(xf):
    return xf * lax.rsqrt(jnp.mean(xf * xf, axis=-1, keepdims=True) + EPS)


def _dot(a, b):
    return jnp.dot(a, b, preferred_element_type=F32)


def _ada_kernel(c_ref, w_ref, b_ref, o_ref):
    cond = jax.nn.silu(c_ref[...]).astype(BF16)
    o_ref[0] = _dot(cond, w_ref[0].astype(BF16)) + b_ref[0]


def _ada_call(c, w_ada, b_ada):
    n_col = 6
    return pl.pallas_call(
        _ada_kernel,
        grid=(DEPTH, n_col),
        in_specs=[
            pl.BlockSpec((BATCH, D_MODEL), lambda l, j: (0, 0)),
            pl.BlockSpec((1, D_MODEL, D_MODEL), lambda l, j: (l, 0, j)),
            pl.BlockSpec((1, 1, D_MODEL), lambda l, j: (l, 0, j)),
        ],
        out_specs=pl.BlockSpec((1, BATCH, D_MODEL), lambda l, j: (l, 0, j)),
        out_shape=jax.ShapeDtypeStruct((DEPTH, BATCH, n_col * D_MODEL), F32),
        compiler_params=pltpu.CompilerParams(
            dimension_semantics=("arbitrary", "arbitrary"), vmem_limit_bytes=VMEM_LIMIT),
        name="ada_mod",
    )(c, w_ada, b_ada.reshape(DEPTH, 1, n_col * D_MODEL))


def _proj_kernel(x_ref, mod_ref, g_ref, w_ref, o_ref):
    shift = mod_ref[0, 0:1, :]
    scale = mod_ref[0, 1:2, :]
    h = (_rms(x_ref[...]) * g_ref[...]) * (1.0 + scale) + shift
    hb = h.astype(BF16)
    for n in range(IN_W // PROJ_TN):
        cols = slice(n * PROJ_TN, (n + 1) * PROJ_TN)
        o_ref[:, cols] = _dot(hb, w_ref[:, cols]).astype(BF16)


def _proj_call(x, mod_l, g, w_in_l):
    tiles_per_b = SEQ // PROJ_TM
    return pl.pallas_call(
        _proj_kernel,
        grid=(N_TOK // PROJ_TM,),
        in_specs=[
            pl.BlockSpec((PROJ_TM, D_MODEL), lambda i: (i, 0)),
            pl.BlockSpec((1, 6, D_MODEL), lambda i: (i // tiles_per_b, 0, 0)),
            _const_spec((1, D_MODEL)),
            _const_spec((D_MODEL, IN_W)),
        ],
        out_specs=pl.BlockSpec((PROJ_TM, IN_W), lambda i: (i, 0)),
        out_shape=jax.ShapeDtypeStruct((N_TOK, IN_W), BF16),
        compiler_params=pltpu.CompilerParams(
            dimension_semantics=("arbitrary",), vmem_limit_bytes=VMEM_LIMIT),
        name="in_proj",
    )(x, mod_l, g, w_in_l)


def _top2_sum(a, b, c, d):
    hi1, lo1 = jnp.maximum(a, b), jnp.minimum(a, b)
    hi2, lo2 = jnp.maximum(c, d), jnp.minimum(c, d)
    return jnp.maximum(hi1, hi2) + jnp.maximum(jnp.minimum(hi1, hi2), jnp.maximum(lo1, lo2))


def _first_argmax(vals):
    best = vals[0]
    idx = jnp.zeros(best.shape, jnp.int32)
    for j in range(1, len(vals)):
        better = vals[j] > best
        idx = jnp.where(better, j, idx)
        best = jnp.where(better, vals[j], best)
    return idx, best


def _mix_kernel(gamma_c,
                proj_ref, x_ref, mod_ref, n2g_ref, cq_ref, sq_ref, ck_ref, sk_ref,
                dec_ref, xi_ref, zeta_ref, wro_ref, pw_ref, ps_ref, wpo_ref, wout_ref,
                wrt_ref, br_ref, tri_ref,
                xo_ref, h2_ref, e_ref, r_ref, w_ref, cnt_ref,
                state_ref, halo_ref, gated_ref, carry_ref):
    b = pl.program_id(0)
    s = pl.program_id(1)
    ts = MIX_TS

    @pl.when(s == 0)
    def _():
        state_ref[...] = jnp.zeros_like(state_ref)
        halo_ref[...] = jnp.zeros_like(halo_ref)

    @pl.when((b == 0) & (s == 0))
    def _():
        carry_ref[...] = jnp.zeros_like(carry_ref)

    for j in range(ts // RET_C):
        rows = slice(j * RET_C, (j + 1) * RET_C)
        cq, sq = cq_ref[rows, :], sq_ref[rows, :]
        ck, sk = ck_ref[rows, :], sk_ref[rows, :]
        for h in range(RET_HEADS):
            qh = proj_ref[rows, OFF_Q + h * RET_QK_DIM:OFF_Q + (h + 1) * RET_QK_DIM].astype(F32)
            kh = proj_ref[rows, OFF_K + h * RET_QK_DIM:OFF_K + (h + 1) * RET_QK_DIM].astype(F32)
            vh = proj_ref[rows, OFF_V + h * RET_V_DIM:OFF_V + (h + 1) * RET_V_DIM]
            qh = qh * cq + pltpu.roll(qh, RET_QK_DIM // 2, axis=1) * sq
            kh = kh * ck + pltpu.roll(kh, RET_QK_DIM // 2, axis=1) * sk
            sc = lax.dot_general(qh.astype(BF16), kh.astype(BF16), (((1,), (1,)), ((), ())),
                                 preferred_element_type=F32) * dec_ref[h]
            st = state_ref[h]
            y = _dot(sc.astype(BF16), vh) + _dot((qh * xi_ref[h]).astype(BF16), st.astype(BF16))
            kv = lax.dot_general((kh * zeta_ref[h]).astype(BF16), vh, (((0,), (0,)), ((), ())),
                                 preferred_element_type=F32)
            state_ref[h] = st * gamma_c[h] + kv
            gh = proj_ref[rows, OFF_G + h * RET_V_DIM:OFF_G + (h + 1) * RET_V_DIM].astype(F32)
            gated_ref[rows, h * RET_V_DIM:(h + 1) * RET_V_DIM] = (jax.nn.silu(gh) * _rms(y)).astype(BF16)
    y_ret = _dot(gated_ref[...], wro_ref[...])

    u = proj_ref[:, OFF_U:OFF_U + POOL_W].astype(F32)
    ext = jnp.concatenate([halo_ref[...], u], axis=0)
    halo_ref[...] = u[ts - POOL_HALO:, :]
    pos = s * ts + lax.broadcasted_iota(jnp.int32, (ts, 1), 0)
    pooled = []
    for g, win in enumerate(POOL_WINDOWS):
        cols = slice(g * POOL_GROUP_DIM, (g + 1) * POOL_GROUP_DIM)
        acc = ext[:, cols]
        shift = 1
        while shift < win:
            acc = acc + pltpu.roll(acc, shift, axis=0)
            shift *= 2
        count = jnp.minimum(pos + 1, win).astype(F32)
        mix = acc[POOL_HALO:, :] / count - u[:, cols]
        pooled.append(_dot(mix.astype(BF16), pw_ref[g]))
    y_pool = _dot((jnp.concatenate(pooled, axis=1) * ps_ref[...]).astype(BF16), wpo_ref[...])

    a_ret = proj_ref[:, OFF_AR:OFF_AR + D_MODEL].astype(F32)
    a_pool = proj_ref[:, OFF_AP:OFF_AP + D_MODEL].astype(F32)
    merged = jax.nn.sigmoid(a_ret) * y_ret + jax.nn.sigmoid(a_pool) * y_pool
    x_new = x_ref[...] + mod_ref[0, 2:3, :] * _dot(merged.astype(BF16), wout_ref[...])
    xo_ref[...] = x_new

    h2 = (_rms(x_new) * n2g_ref[...]) * (1.0 + mod_ref[0, 4:5, :]) + mod_ref[0, 3:4, :]
    h2_ref[...] = h2

    logits = lax.dot_general(wrt_ref[...], h2.astype(BF16), (((1,), (1,)), ((), ())),
                             preferred_element_type=F32) + br_ref[...]
    p = jnp.exp(logits - jnp.max(logits, axis=0, keepdims=True))
    probs = p / jnp.sum(p, axis=0, keepdims=True)
    pr = [probs[e:e + 1, :] for e in range(N_EXPERTS)]
    scores = [_top2_sum(*pr[g * EXPERTS_PER_GROUP:(g + 1) * EXPERTS_PER_GROUP]) for g in range(N_GROUPS)]
    grp, _ = _first_argmax(scores)
    in_grp = []
    for jj in range(EXPERTS_PER_GROUP):
        v = pr[jj]
        for g in range(1, N_GROUPS):
            v = jnp.where(grp == g, pr[g * EXPERTS_PER_GROUP + jj], v)
        in_grp.append(v)
    i0, p0 = _first_argmax(in_grp)
    i1, p1 = _first_argmax([jnp.where(i0 == jj, -1.0, in_grp[jj]) for jj in range(EXPERTS_PER_GROUP)])
    e0 = grp * EXPERTS_PER_GROUP + i0
    e1 = grp * EXPERTS_PER_GROUP + i1
    den = p0 + p1
    e_ref[...] = jnp.concatenate([e0, e1], axis=0)
    w_ref[...] = jnp.concatenate([p0 / den, p1 / den], axis=0)

    erow = lax.broadcasted_iota(jnp.int32, (N_EXPERTS, ts), 0)
    oh0 = erow == e0
    oh1 = erow == e1
    ohf = jnp.where(oh0 | oh1, 1.0, 0.0)
    before = _dot(ohf.astype(BF16), tri_ref[...]) + carry_ref[:, 0:1]
    r0 = jnp.sum(jnp.where(oh0, before, 0.0), axis=0, keepdims=True)
    r1 = jnp.sum(jnp.where(oh1, before, 0.0), axis=0, keepdims=True)
    r_ref[...] = jnp.concatenate([r0, r1], axis=0).astype(jnp.int32)
    carry_ref[...] = carry_ref[...] + jnp.sum(ohf, axis=1, keepdims=True)
    cnt_ref[...] = carry_ref[...]


def _retention_tables():
    lg = np.log(1.0 - 2.0 ** (-5.0 - np.arange(RET_HEADS, dtype=np.float32))).astype(np.float32)
    idx = np.arange(RET_C, dtype=np.float32)
    rel = idx[:, None] - idx[None, :]
    causal = rel >= 0
    dec = np.where(causal[None], np.exp(lg[:, None, None] * np.where(causal, rel, 0.0)[None]), 0.0)
    xi = np.exp(lg[:, None] * (idx[None, :] + 1.0))
    zeta = np.exp(lg[:, None] * (RET_C - 1.0 - idx[None, :]))
    gamma_c = tuple(float(v) for v in np.exp(lg * RET_C).astype(np.float32))
    xi = np.broadcast_to(xi[:, :, None], (RET_HEADS, RET_C, RET_QK_DIM))
    zeta = np.broadcast_to(zeta[:, :, None], (RET_HEADS, RET_C, RET_QK_DIM))
    return (jnp.asarray(dec, F32), jnp.asarray(xi, F32), jnp.asarray(zeta, F32), gamma_c)


def _rotary_tables():
    half = RET_QK_DIM // 2
    freqs = ROPE_BASE ** (-jnp.arange(0, RET_QK_DIM, 2, dtype=F32) / RET_QK_DIM)
    ang = jnp.arange(SEQ, dtype=F32)[:, None] * freqs[None, :]
    cos, sin = jnp.cos(ang), jnp.sin(ang)
    cos2 = jnp.concatenate([cos, cos], axis=1)
    sin2 = jnp.concatenate([-sin, sin], axis=1)
    k_scale = RET_QK_DIM ** -0.5
    assert cos2.shape == (SEQ, 2 * half)
    return cos2, sin2, cos2 * k_scale, sin2 * k_scale


def _mix_call(proj, x, mod_l, n2g, rot, ret, wro, pw, ps, wpo, wout, wrt, br, tri):
    dec, xi, zeta, gamma_c = ret
    ts = MIX_TS
    n_s = SEQ // ts
    tok = lambda b, s: (b * n_s + s, 0)
    lane_tok = lambda b, s: (0, b * n_s + s)
    seq_tab = pl.BlockSpec((ts, RET_QK_DIM), lambda b, s: (s, 0))
    return pl.pallas_call(
        functools.partial(_mix_kernel, gamma_c),
        grid=(BATCH, n_s),
        in_specs=[
            pl.BlockSpec((ts, IN_W), tok),
            pl.BlockSpec((ts, D_MODEL), tok),
            pl.BlockSpec((1, 6, D_MODEL), lambda b, s: (b, 0, 0)),
            _const_spec((1, D_MODEL)),
            seq_tab, seq_tab, seq_tab, seq_tab,
            _const_spec((RET_HEADS, RET_C, RET_C)),
            _const_spec((RET_HEADS, RET_C, RET_QK_DIM)),
            _const_spec((RET_HEADS, RET_C, RET_QK_DIM)),
            _const_spec((RET_V_W, D_MODEL)),
            _const_spec((POOL_GROUPS, POOL_GROUP_DIM, POOL_GROUP_DIM)),
            _const_spec((1, POOL_W)),
            _const_spec((POOL_W, D_MODEL)),
            _const_spec((D_MODEL, D_MODEL)),
            _const_spec((N_EXPERTS, D_MODEL)),
            _const_spec((N_EXPERTS, 1)),
            _const_spec((ts, ts)),
        ],
        out_specs=[
            pl.BlockSpec((ts, D_MODEL), tok),
            pl.BlockSpec((ts, D_MODEL), tok),
            pl.BlockSpec((TOP_K, ts), lane_tok),
            pl.BlockSpec((TOP_K, ts), lane_tok),
            pl.BlockSpec((TOP_K, ts), lane_tok),
            pl.BlockSpec((N_EXPERTS, 128), lambda b, s: (0, 0)),
        ],
        out_shape=[
            jax.ShapeDtypeStruct((N_TOK, D_MODEL), F32),
            jax.ShapeDtypeStruct((N_TOK, D_MODEL), F32),
            jax.ShapeDtypeStruct((TOP_K, N_TOK), jnp.int32),
            jax.ShapeDtypeStruct((TOP_K, N_TOK), jnp.int32),
            jax.ShapeDtypeStruct((TOP_K, N_TOK), F32),
            jax.ShapeDtypeStruct((N_EXPERTS, 128), F32),
        ],
        scratch_shapes=[
            pltpu.VMEM((RET_HEADS, RET_QK_DIM, RET_V_DIM), F32),
            pltpu.VMEM((POOL_HALO, POOL_W), F32),
            pltpu.VMEM((ts, RET_V_W), BF16),
            pltpu.VMEM((N_EXPERTS, 128), F32),
        ],
        compiler_params=pltpu.CompilerParams(
            dimension_semantics=("arbitrary", "arbitrary"), vmem_limit_bytes=VMEM_LIMIT),
        name="mixer_route",
    )(proj, x, mod_l, n2g, *rot, dec, xi, zeta, wro, pw, ps, wpo, wout, wrt, br, tri)


def _dispatch_kernel(d0_ref, d1_ref, h_ref, xs_in_ref, xs_ref, sem):
    del xs_in_ref

    def row_copy(t, d):
        return pltpu.make_async_copy(h_ref.at[pl.ds(t, 1), :], xs_ref.at[pl.ds(d, 1), :], sem)

    def issue(t, carry):
        row_copy(t, d0_ref[t]).start()
        row_copy(t, d1_ref[t]).start()
        return carry

    lax.fori_loop(0, DISP_TD, issue, 0)

    def drain(t, carry):
        row_copy(t, d0_ref[t]).wait()
        row_copy(t, d1_ref[t]).wait()
        return carry

    lax.fori_loop(0, DISP_TD, drain, 0)


def _dispatch_call(dest0, dest1, h2, xs):
    return pl.pallas_call(
        _dispatch_kernel,
        grid=(N_TOK // DISP_TD,),
        in_specs=[
            pl.BlockSpec((DISP_TD,), lambda i: (i,), memory_space=pltpu.SMEM),
            pl.BlockSpec((DISP_TD,), lambda i: (i,), memory_space=pltpu.SMEM),
            pl.BlockSpec((DISP_TD, D_MODEL), lambda i: (i, 0)),
            pl.BlockSpec(memory_space=pl.ANY),
        ],
        out_specs=pl.BlockSpec(memory_space=pl.ANY),
        out_shape=jax.ShapeDtypeStruct((N_SLOTS, D_MODEL), F32),
        scratch_shapes=[pltpu.SemaphoreType.DMA(())],
        input_output_aliases={3: 0},
        compiler_params=pltpu.CompilerParams(
            dimension_semantics=("arbitrary",), vmem_limit_bytes=VMEM_LIMIT, has_side_effects=True),
        name="moe_dispatch",
    )(dest0, dest1, h2, xs)


def _expert_kernel(be_ref, nu_ref, xs_ref, wg_ref, wu_ref, wd_ref, o_ref):
    i = pl.program_id(0)

    @pl.when(i < nu_ref[0])
    def _():
        xb = xs_ref[...].astype(BF16)
        a = _dot(xb, wg_ref[0])
        u = _dot(xb, wu_ref[0])
        o_ref[...] = _dot((jax.nn.silu(a) * u).astype(BF16), wd_ref[0])

    @pl.when(i >= nu_ref[0])
    def _():
        o_ref[...] = jnp.zeros_like(o_ref)


def _expert_call(block_e, n_used, xs, wg, wu, wd):
    def xs_map(i, be, nu):
        return (jnp.minimum(i, nu[0] - 1), 0)

    def w_map(i, be, nu):
        return (be[i], 0, 0)

    grid_spec = pltpu.PrefetchScalarGridSpec(
        num_scalar_prefetch=2,
        grid=(N_BLOCKS,),
        in_specs=[
            pl.BlockSpec((EXP_BLK, D_MODEL), xs_map),
            pl.BlockSpec((1, D_MODEL, D_EXPERT), w_map),
            pl.BlockSpec((1, D_MODEL, D_EXPERT), w_map),
            pl.BlockSpec((1, D_EXPERT, D_MODEL), w_map),
        ],
        out_specs=pl.BlockSpec((EXP_BLK, D_MODEL), lambda i, be, nu: (i, 0)),
    )
    return pl.pallas_call(
        _expert_kernel,
        grid_spec=grid_spec,
        out_shape=jax.ShapeDtypeStruct((N_SLOTS, D_MODEL), F32),
        compiler_params=pltpu.CompilerParams(
            dimension_semantics=("arbitrary",), vmem_limit_bytes=VMEM_LIMIT),
        name="moe_experts",
    )(block_e, n_used, xs, wg, wu, wd)


def _combine_kernel(final, d0_ref, d1_ref, ys_ref, x_ref, w_ref, mod_ref, fg_ref, o_ref, buf_ref, sem):
    def row_copy(t, d, k):
        return pltpu.make_async_copy(ys_ref.at[pl.ds(d, 1), :], buf_ref.at[k, pl.ds(t, 1), :], sem)

    def issue(t, carry):
        row_copy(t, d0_ref[t], 0).start()
        row_copy(t, d1_ref[t], 1).start()
        return carry

    lax.fori_loop(0, COMB_TC, issue, 0)

    def drain(t, carry):
        row_copy(t, d0_ref[t], 0).wait()
        row_copy(t, d1_ref[t], 1).wait()
        return carry

    lax.fori_loop(0, COMB_TC, drain, 0)

    w = w_ref[...]
    y = buf_ref[0] * w[:, 0:1] + buf_ref[1] * w[:, 1:2]
    x_new = x_ref[...] + mod_ref[0, 5:6, :] * y
    if final:
        x_new = _rms(x_new) * fg_ref[...]
    o_ref[...] = x_new


def _combine_call(dest0, dest1, ys, x, wts_t, mod_l, final_g, final):
    tiles_per_b = SEQ // COMB_TC
    return pl.pallas_call(
        functools.partial(_combine_kernel, final),
        grid=(N_TOK // COMB_TC,),
        in_specs=[
            pl.BlockSpec((COMB_TC,), lambda i: (i,), memory_space=pltpu.SMEM),
            pl.BlockSpec((COMB_TC,), lambda i: (i,), memory_space=pltpu.SMEM),
            pl.BlockSpec(memory_space=pl.ANY),
            pl.BlockSpec((COMB_TC, D_MODEL), lambda i: (i, 0)),
            pl.BlockSpec((COMB_TC, TOP_K), lambda i: (i, 0)),
            pl.BlockSpec((1, 6, D_MODEL), lambda i: (i // tiles_per_b, 0, 0)),
            _const_spec((1, D_MODEL)),
        ],
        out_specs=pl.BlockSpec((COMB_TC, D_MODEL), lambda i: (i, 0)),
        out_shape=jax.ShapeDtypeStruct((N_TOK, D_MODEL), F32),
        scratch_shapes=[pltpu.VMEM((TOP_K, COMB_TC, D_MODEL), F32), pltpu.SemaphoreType.DMA(())],
        compiler_params=pltpu.CompilerParams(
            dimension_semantics=("arbitrary",), vmem_limit_bytes=VMEM_LIMIT),
        name="moe_combine",
    )(dest0, dest1, ys, x, wts_t, mod_l, final_g)


def kernel(x, c, w_ada, b_ada, norm1_g, norm2_g, w_in, w_ret_o, pool_w, pool_scale, w_pool_o, w_out,
           w_router, b_router, w_e_gate, w_e_up, w_e_down, final_g):
    assert x.shape == (BATCH, SEQ, D_MODEL) and w_in.shape == (DEPTH, D_MODEL, IN_W)
    xt = x.reshape(N_TOK, D_MODEL)
    mod = _ada_call(c, w_ada, b_ada).reshape(DEPTH, BATCH, 6, D_MODEL)

    rot = _rotary_tables()
    ret = _retention_tables()
    tri = (jnp.arange(MIX_TS)[:, None] < jnp.arange(MIX_TS)[None, :]).astype(BF16)
    wrt = w_router.T.astype(BF16)
    br = b_router.astype(F32).reshape(N_EXPERTS, 1)
    xs = jnp.zeros((N_SLOTS, D_MODEL), F32)
    fg = final_g.reshape(1, D_MODEL)

    for l in range(DEPTH):
        proj = _proj_call(xt, mod[l], norm1_g[l].reshape(1, D_MODEL), w_in[l].astype(BF16))
        xt, h2, eidx, rank, wts, counts = _mix_call(
            proj, xt, mod[l], norm2_g[l].reshape(1, D_MODEL), rot, ret,
            w_ret_o[l].astype(BF16), pool_w[l].astype(BF16), pool_scale[l].reshape(1, POOL_W),
            w_pool_o[l].astype(BF16), w_out[l].astype(BF16), wrt, br, tri)

        cnt = counts[:, 0].astype(jnp.int32)
        padded = ((cnt + EXP_BLK - 1) // EXP_BLK) * EXP_BLK
        pad_end = jnp.cumsum(padded)
        pad_start = pad_end - padded
        dest = pad_start[eidx] + rank
        block_first = jnp.arange(N_BLOCKS, dtype=jnp.int32) * EXP_BLK
        block_e = jnp.minimum(jnp.searchsorted(pad_end, block_first, side="right"),
                              N_EXPERTS - 1).astype(jnp.int32)
        n_used = (pad_end[-1:] // EXP_BLK).astype(jnp.int32)

        xs = _dispatch_call(dest[0], dest[1], h2, xs)
        ys = _expert_call(block_e, n_used, xs, w_e_gate[l].astype(BF16), w_e_up[l].astype(BF16),
                          w_e_down[l].astype(BF16))
        xt = _combine_call(dest[0], dest[1], ys, xt, wts.T, mod[l], fg, l == DEPTH - 1)
    return xt.reshape(BATCH, SEQ, D_MODEL)
```

```python
import functools

import numpy as np
import jax
import jax.numpy as jnp
from jax import lax
from jax.experimental import pallas as pl
from jax.experimental.pallas import tpu as pltpu

D_MODEL = 1024
BATCH = 16
SEQ = 2048
DEPTH = 4
N_TOK = BATCH * SEQ

RET_HEADS = 4
RET_QK_DIM = 128
RET_V_DIM = 256
RET_QK_W = RET_HEADS * RET_QK_DIM
RET_V_W = RET_HEADS * RET_V_DIM
ROPE_BASE = 10000.0
POOL_WINDOWS = (2, 4, 8, 16)
POOL_GROUPS = 4
POOL_GROUP_DIM = 128
POOL_W = POOL_GROUPS * POOL_GROUP_DIM
POOL_HALO = 16
IN_W = 2 * RET_QK_W + 2 * RET_V_W + POOL_W + 2 * D_MODEL
OFF_Q = 0
OFF_K = OFF_Q + RET_QK_W
OFF_V = OFF_K + RET_QK_W
OFF_G = OFF_V + RET_V_W
OFF_U = OFF_G + RET_V_W
OFF_AR = OFF_U + POOL_W
OFF_AP = OFF_AR + D_MODEL

N_EXPERTS = 16
N_GROUPS = 4
EXPERTS_PER_GROUP = 4
TOP_K = 2
D_EXPERT = D_MODEL // 2
EPS = 1e-6

TILE = 512
PROJ_TN = 512
RET_C = 256
N_TILES = N_TOK // TILE
ALIGN = 16
SORT_R = -(-(TOP_K * TILE + N_EXPERTS * (ALIGN - 1)) // 128) * 128
SORT_TN = 256
EXP_BLK = 256
UNITS_PER_BLK = EXP_BLK // ALIGN
MAX_UNITS = N_TILES * ((TOP_K * TILE + N_EXPERTS * (ALIGN - 1)) // ALIGN)
N_BLOCKS = -(-MAX_UNITS // UNITS_PER_BLK) + N_EXPERTS
VMEM_LIMIT = 56 * 1024 * 1024

F32 = jnp.float32
BF16 = jnp.bfloat16
I32 = jnp.int32


def _const_spec(shape):
    zeros = (0,) * len(shape)
    return pl.BlockSpec(shape, lambda *_: zeros, pipeline_mode=pl.Buffered(1))


def _rms(xf):
    return xf * lax.rsqrt(jnp.mean(xf * xf, axis=-1, keepdims=True) + EPS)


def _dot(a, b):
    return jnp.dot(a, b, preferred_element_type=F32)


def _ada_kernel(c_ref, w_ref, b_ref, o_ref):
    cond = jax.nn.silu(c_ref[...]).astype(BF16)
    o_ref[0] = _dot(cond, w_ref[0].astype(BF16)) + b_ref[0]


def _ada_call(c, w_ada, b_ada):
    n_col = 6
    return pl.pallas_call(
        _ada_kernel,
        grid=(DEPTH, n_col),
        in_specs=[
            pl.BlockSpec((BATCH, D_MODEL), lambda l, j: (0, 0)),
            pl.BlockSpec((1, D_MODEL, D_MODEL), lambda l, j: (l, 0, j)),
            pl.BlockSpec((1, 1, D_MODEL), lambda l, j: (l, 0, j)),
        ],
        out_specs=pl.BlockSpec((1, BATCH, D_MODEL), lambda l, j: (l, 0, j)),
        out_shape=jax.ShapeDtypeStruct((DEPTH, BATCH, n_col * D_MODEL), F32),
        compiler_params=pltpu.CompilerParams(
            dimension_semantics=("arbitrary", "arbitrary"), vmem_limit_bytes=VMEM_LIMIT),
        name="ada_mod",
    )(c, w_ada, b_ada.reshape(DEPTH, 1, n_col * D_MODEL))


def _moe_combine(x, cols, ys, gate):
    lane = lax.broadcasted_iota(I32, (TILE, SORT_R), 1)
    l0 = cols[:, 0:1].astype(I32)
    l1 = cols[:, 1:2].astype(I32)
    pw = jnp.where(lane == l0, cols[:, 2:3], 0.0) + jnp.where(lane == l1, cols[:, 3:4], 0.0)
    return x + gate * _dot(pw.astype(BF16), ys)


def _proj_kernel(combine, *refs):
    if combine:
        x_ref, cols_ref, ys_ref, modp_ref, mod_ref, g_ref, w_ref, xo_ref, o_ref = refs
        x = _moe_combine(x_ref[...], cols_ref[...], ys_ref[0], modp_ref[0, 5:6, :])
        xo_ref[...] = x
    else:
        x_ref, mod_ref, g_ref, w_ref, o_ref = refs
        x = x_ref[...]
    h = (_rms(x) * g_ref[...]) * (1.0 + mod_ref[0, 1:2, :]) + mod_ref[0, 0:1, :]
    hb = h.astype(BF16)
    for n in range(IN_W // PROJ_TN):
        cols = slice(n * PROJ_TN, (n + 1) * PROJ_TN)
        o_ref[:, cols] = _dot(hb, w_ref[:, cols]).astype(BF16)


def _proj_call(x, mod_l, g, w_in_l, moe=None):
    tiles_per_b = SEQ // TILE
    tok = pl.BlockSpec((TILE, D_MODEL), lambda i: (i, 0))
    mod_spec = pl.BlockSpec((1, 6, D_MODEL), lambda i: (i // tiles_per_b, 0, 0))
    proj_spec = pl.BlockSpec((TILE, IN_W), lambda i: (i, 0))
    proj_shape = jax.ShapeDtypeStruct((N_TOK, IN_W), BF16)
    tail_specs = [mod_spec, _const_spec((1, D_MODEL)), _const_spec((D_MODEL, IN_W))]
    if moe is None:
        in_specs = [tok] + tail_specs
        args = (x, mod_l, g, w_in_l)
        out_specs, out_shape = proj_spec, proj_shape
    else:
        cols, ys, mod_prev = moe
        in_specs = [tok, pl.BlockSpec((TILE, 4), lambda i: (i, 0)),
                    pl.BlockSpec((1, SORT_R, D_MODEL), lambda i: (i, 0, 0)), mod_spec] + tail_specs
        args = (x, cols, ys, mod_prev, mod_l, g, w_in_l)
        out_specs = [tok, proj_spec]
        out_shape = [jax.ShapeDtypeStruct((N_TOK, D_MODEL), F32), proj_shape]
    return pl.pallas_call(
        functools.partial(_proj_kernel, moe is not None),
        grid=(N_TILES,),
        in_specs=in_specs,
        out_specs=out_specs,
        out_shape=out_shape,
        compiler_params=pltpu.CompilerParams(
            dimension_semantics=("arbitrary",), vmem_limit_bytes=VMEM_LIMIT),
        name="in_proj",
    )(*args)


def _final_kernel(x_ref, cols_ref, ys_ref, modp_ref, g_ref, o_ref):
    x = _moe_combine(x_ref[...], cols_ref[...], ys_ref[0], modp_ref[0, 5:6, :])
    o_ref[...] = _rms(x) * g_ref[...]


def _final_call(x, cols, ys, mod_prev, final_g):
    tiles_per_b = SEQ // TILE
    tok = pl.BlockSpec((TILE, D_MODEL), lambda i: (i, 0))
    return pl.pallas_call(
        _final_kernel,
        grid=(N_TILES,),
        in_specs=[tok, pl.BlockSpec((TILE, 4), lambda i: (i, 0)),
                  pl.BlockSpec((1, SORT_R, D_MODEL), lambda i: (i, 0, 0)),
                  pl.BlockSpec((1, 6, D_MODEL), lambda i: (i // tiles_per_b, 0, 0)),
                  _const_spec((1, D_MODEL))],
        out_specs=tok,
        out_shape=jax.ShapeDtypeStruct((N_TOK, D_MODEL), F32),
        compiler_params=pltpu.CompilerParams(
            dimension_semantics=("arbitrary",), vmem_limit_bytes=VMEM_LIMIT),
        name="final_norm",
    )(x, cols, ys, mod_prev, final_g)


def _top2_sum(a, b, c, d):
    hi1, lo1 = jnp.maximum(a, b), jnp.minimum(a, b)
    hi2, lo2 = jnp.maximum(c, d), jnp.minimum(c, d)
    return jnp.maximum(hi1, hi2) + jnp.maximum(jnp.minimum(hi1, hi2), jnp.maximum(lo1, lo2))


def _first_argmax(vals):
    best = vals[0]
    idx = jnp.zeros(best.shape, I32)
    for j in range(1, len(vals)):
        better = vals[j] > best
        idx = jnp.where(better, j, idx)
        best = jnp.where(better, vals[j], best)
    return idx, best


def _mix_kernel(gamma_c,
                proj_ref, x_ref, mod_ref, n2g_ref, cq_ref, sq_ref, ck_ref, sk_ref,
                dec_ref, xi_ref, zeta_ref, wro_ref, pw_ref, ps_ref, wpo_ref, wout_ref,
                wrt_ref, br_ref, tri_ref,
                xo_ref, xs_ref, lw_ref, np_ref,
                state_ref, halo_ref, gated_ref):
    s = pl.program_id(1)
    ts = TILE

    @pl.when(s == 0)
    def _():
        state_ref[...] = jnp.zeros_like(state_ref)
        halo_ref[...] = jnp.zeros_like(halo_ref)

    for j in range(ts // RET_C):
        rows = slice(j * RET_C, (j + 1) * RET_C)
        cq, sq = cq_ref[rows, :], sq_ref[rows, :]
        ck, sk = ck_ref[rows, :], sk_ref[rows, :]
        for h in range(RET_HEADS):
            qh = proj_ref[rows, OFF_Q + h * RET_QK_DIM:OFF_Q + (h + 1) * RET_QK_DIM].astype(F32)
            kh = proj_ref[rows, OFF_K + h * RET_QK_DIM:OFF_K + (h + 1) * RET_QK_DIM].astype(F32)
            vh = proj_ref[rows, OFF_V + h * RET_V_DIM:OFF_V + (h + 1) * RET_V_DIM]
            qh = qh * cq + pltpu.roll(qh, RET_QK_DIM // 2, axis=1) * sq
            kh = kh * ck + pltpu.roll(kh, RET_QK_DIM // 2, axis=1) * sk
            sc = lax.dot_general(qh.astype(BF16), kh.astype(BF16), (((1,), (1,)), ((), ())),
                                 preferred_element_type=F32) * dec_ref[h]
            st = state_ref[h]
            y = _dot(sc.astype(BF16), vh) + _dot((qh * xi_ref[h]).astype(BF16), st.astype(BF16))
            kv = lax.dot_general((kh * zeta_ref[h]).astype(BF16), vh, (((0,), (0,)), ((), ())),
                                 preferred_element_type=F32)
            state_ref[h] = st * gamma_c[h] + kv
            gh = proj_ref[rows, OFF_G + h * RET_V_DIM:OFF_G + (h + 1) * RET_V_DIM].astype(F32)
            gated_ref[rows, h * RET_V_DIM:(h + 1) * RET_V_DIM] = (jax.nn.silu(gh) * _rms(y)).astype(BF16)
    y_ret = _dot(gated_ref[...], wro_ref[...])

    u = proj_ref[:, OFF_U:OFF_U + POOL_W].astype(F32)
    ext = jnp.concatenate([halo_ref[...], u], axis=0)
    halo_ref[...] = u[ts - POOL_HALO:, :]
    pos = s * ts + lax.broadcasted_iota(I32, (ts, 1), 0)
    pooled = []
    for g, win in enumerate(POOL_WINDOWS):
        cols = slice(g * POOL_GROUP_DIM, (g + 1) * POOL_GROUP_DIM)
        acc = ext[:, cols]
        shift = 1
        while shift < win:
            acc = acc + pltpu.roll(acc, shift, axis=0)
            shift *= 2
        count = jnp.minimum(pos + 1, win).astype(F32)
        mix = acc[POOL_HALO:, :] / count - u[:, cols]
        pooled.append(_dot(mix.astype(BF16), pw_ref[g]))
    y_pool = _dot((jnp.concatenate(pooled, axis=1) * ps_ref[...]).astype(BF16), wpo_ref[...])

    a_ret = proj_ref[:, OFF_AR:OFF_AR + D_MODEL].astype(F32)
    a_pool = proj_ref[:, OFF_AP:OFF_AP + D_MODEL].astype(F32)
    merged = jax.nn.sigmoid(a_ret) * y_ret + jax.nn.sigmoid(a_pool) * y_pool
    x_new = x_ref[...] + mod_ref[0, 2:3, :] * _dot(merged.astype(BF16), wout_ref[...])
    xo_ref[...] = x_new

    h2 = (_rms(x_new) * n2g_ref[...]) * (1.0 + mod_ref[0, 4:5, :]) + mod_ref[0, 3:4, :]
    h2b = h2.astype(BF16)

    logits = lax.dot_general(wrt_ref[...], h2b, (((1,), (1,)), ((), ())),
                             preferred_element_type=F32) + br_ref[...]
    p = jnp.exp(logits - jnp.max(logits, axis=0, keepdims=True))
    probs = p / jnp.sum(p, axis=0, keepdims=True)
    pr = [probs[e:e + 1, :] for e in range(N_EXPERTS)]
    scores = [_top2_sum(*pr[g * EXPERTS_PER_GROUP:(g + 1) * EXPERTS_PER_GROUP]) for g in range(N_GROUPS)]
    grp, _ = _first_argmax(scores)
    in_grp = []
    for jj in range(EXPERTS_PER_GROUP):
        v = pr[jj]
        for g in range(1, N_GROUPS):
            v = jnp.where(grp == g, pr[g * EXPERTS_PER_GROUP + jj], v)
        in_grp.append(v)
    i0, p0 = _first_argmax(in_grp)
    i1, p1 = _first_argmax([jnp.where(i0 == jj, -1.0, in_grp[jj]) for jj in range(EXPERTS_PER_GROUP)])
    e0 = grp * EXPERTS_PER_GROUP + i0
    e1 = grp * EXPERTS_PER_GROUP + i1
    den = p0 + p1

    erow = lax.broadcasted_iota(I32, (N_EXPERTS, ts), 0)
    oh0 = erow == e0
    oh1 = erow == e1
    ohf = jnp.where(oh0 | oh1, 1.0, 0.0)
    n_e = jnp.sum(ohf, axis=1, keepdims=True)
    n_pad = jnp.broadcast_to(jnp.floor((n_e + (ALIGN - 1.0)) * (1.0 / ALIGN)) * ALIGN, (N_EXPERTS, 128))
    np_ref[0] = n_pad
    erow128 = lax.broadcasted_iota(I32, (N_EXPERTS, 128), 0)
    run_end = n_pad
    shift = 1
    while shift < N_EXPERTS:
        run_end = run_end + jnp.where(erow128 >= shift, pltpu.roll(run_end, shift, axis=0), 0.0)
        shift *= 2
    run_start = (run_end - n_pad)[:, 0:1]
    row_of = _dot(ohf.astype(BF16), tri_ref[...]) + run_start
    l0 = jnp.sum(jnp.where(oh0, row_of, 0.0), axis=0, keepdims=True)
    l1 = jnp.sum(jnp.where(oh1, row_of, 0.0), axis=0, keepdims=True)
    lw_ref[...] = jnp.concatenate([l0, l1, p0 / den, p1 / den], axis=0)

    srow = lax.broadcasted_iota(I32, (SORT_R, ts), 0)
    perm = jnp.where((srow == l0.astype(I32)) | (srow == l1.astype(I32)), 1.0, 0.0).astype(BF16)
    for n in range(D_MODEL // SORT_TN):
        cols = slice(n * SORT_TN, (n + 1) * SORT_TN)
        xs_ref[0, :, cols] = _dot(perm, h2b[:, cols]).astype(BF16)


def _retention_tables():
    lg = np.log(1.0 - 2.0 ** (-5.0 - np.arange(RET_HEADS, dtype=np.float32))).astype(np.float32)
    idx = np.arange(RET_C, dtype=np.float32)
    rel = idx[:, None] - idx[None, :]
    causal = rel >= 0
    dec = np.where(causal[None], np.exp(lg[:, None, None] * np.where(causal, rel, 0.0)[None]), 0.0)
    xi = np.exp(lg[:, None] * (idx[None, :] + 1.0))
    zeta = np.exp(lg[:, None] * (RET_C - 1.0 - idx[None, :]))
    gamma_c = tuple(float(v) for v in np.exp(lg * RET_C).astype(np.float32))
    xi = np.broadcast_to(xi[:, :, None], (RET_HEADS, RET_C, RET_QK_DIM))
    zeta = np.broadcast_to(zeta[:, :, None], (RET_HEADS, RET_C, RET_QK_DIM))
    return (jnp.asarray(dec, F32), jnp.asarray(xi, F32), jnp.asarray(zeta, F32), gamma_c)


def _rotary_tables():
    freqs = ROPE_BASE ** (-jnp.arange(0, RET_QK_DIM, 2, dtype=F32) / RET_QK_DIM)
    ang = jnp.arange(SEQ, dtype=F32)[:, None] * freqs[None, :]
    cos, sin = jnp.cos(ang), jnp.sin(ang)
    cos2 = jnp.concatenate([cos, cos], axis=1)
    sin2 = jnp.concatenate([-sin, sin], axis=1)
    k_scale = RET_QK_DIM ** -0.5
    return cos2, sin2, cos2 * k_scale, sin2 * k_scale


def _mix_call(proj, x, mod_l, n2g, rot, ret, wro, pw, ps, wpo, wout, wrt, br, tri):
    dec, xi, zeta, gamma_c = ret
    ts = TILE
    n_s = SEQ // ts
    tok = lambda b, s: (b * n_s + s, 0)
    seq_tab = pl.BlockSpec((ts, RET_QK_DIM), lambda b, s: (s, 0))
    return pl.pallas_call(
        functools.partial(_mix_kernel, gamma_c),
        grid=(BATCH, n_s),
        in_specs=[
            pl.BlockSpec((ts, IN_W), tok),
            pl.BlockSpec((ts, D_MODEL), tok),
            pl.BlockSpec((1, 6, D_MODEL), lambda b, s: (b, 0, 0)),
            _const_spec((1, D_MODEL)),
            seq_tab, seq_tab, seq_tab, seq_tab,
            _const_spec((RET_HEADS, RET_C, RET_C)),
            _const_spec((RET_HEADS, RET_C, RET_QK_DIM)),
            _const_spec((RET_HEADS, RET_C, RET_QK_DIM)),
            _const_spec((RET_V_W, D_MODEL)),
            _const_spec((POOL_GROUPS, POOL_GROUP_DIM, POOL_GROUP_DIM)),
            _const_spec((1, POOL_W)),
            _const_spec((POOL_W, D_MODEL)),
            _const_spec((D_MODEL, D_MODEL)),
            _const_spec((N_EXPERTS, D_MODEL)),
            _const_spec((N_EXPERTS, 1)),
            _const_spec((ts, ts)),
        ],
        out_specs=[
            pl.BlockSpec((ts, D_MODEL), tok),
            pl.BlockSpec((1, SORT_R, D_MODEL), lambda b, s: (b * n_s + s, 0, 0)),
            pl.BlockSpec((4, ts), lambda b, s: (0, b * n_s + s)),
            pl.BlockSpec((1, N_EXPERTS, 128), lambda b, s: (b * n_s + s, 0, 0)),
        ],
        out_shape=[
            jax.ShapeDtypeStruct((N_TOK, D_MODEL), F32),
            jax.ShapeDtypeStruct((N_TILES, SORT_R, D_MODEL), BF16),
            jax.ShapeDtypeStruct((4, N_TOK), F32),
            jax.ShapeDtypeStruct((N_TILES, N_EXPERTS, 128), F32),
        ],
        scratch_shapes=[
            pltpu.VMEM((RET_HEADS, RET_QK_DIM, RET_V_DIM), F32),
            pltpu.VMEM((POOL_HALO, POOL_W), F32),
            pltpu.VMEM((ts, RET_V_W), BF16),
        ],
        compiler_params=pltpu.CompilerParams(
            dimension_semantics=("arbitrary", "arbitrary"), vmem_limit_bytes=VMEM_LIMIT),
        name="mixer_route",
    )(proj, x, mod_l, n2g, *rot, dec, xi, zeta, wro, pw, ps, wpo, wout, wrt, br, tri)


def _plan(n_pad):
    n_pad = n_pad.astype(I32)
    units = (n_pad // ALIGN).T
    run_row = (jnp.cumsum(n_pad, axis=1) - n_pad).T
    unit_start = jnp.cumsum(units, axis=1) - units
    total = units.sum(axis=1)
    n_blk = (total + UNITS_PER_BLK - 1) // UNITS_PER_BLK
    blk_end = jnp.cumsum(n_blk)
    blk_start = blk_end - n_blk
    n_used = blk_end[-1:]
    b = jnp.arange(N_BLOCKS, dtype=I32)
    e_b = jnp.minimum((b[:, None] >= blk_end[None, :]).sum(axis=1), N_EXPERTS - 1).astype(I32)
    oh_e = (e_b[:, None] == jnp.arange(N_EXPERTS, dtype=I32)[None, :]).astype(I32)
    pick = lambda tab: (oh_e[:, :, None] * tab[None, :, :]).sum(axis=1)
    unit_start_b, units_b, run_row_b = pick(unit_start), pick(units), pick(run_row)
    total_b = (oh_e * total[None, :]).sum(axis=1)
    first_q = (b - (oh_e * blk_start[None, :]).sum(axis=1)) * UNITS_PER_BLK
    n_valid = jnp.where(b < n_used[0], jnp.clip(total_b - first_q, 0, UNITS_PER_BLK), 0).astype(I32)
    q = first_q[:, None] + jnp.arange(UNITS_PER_BLK, dtype=I32)[None, :]
    tile = jnp.minimum(((unit_start_b + units_b)[:, None, :] <= q[:, :, None]).sum(axis=2), N_TILES - 1)
    oh_t = (tile[:, :, None] == jnp.arange(N_TILES, dtype=I32)[None, None, :]).astype(I32)
    base = jnp.arange(N_TILES, dtype=I32)[None, :] * SORT_R + run_row_b - unit_start_b * ALIGN
    src = q * ALIGN + (oh_t * base[:, None, :]).sum(axis=2)
    valid = jnp.arange(UNITS_PER_BLK, dtype=I32)[None, :] < n_valid[:, None]
    src = jnp.where(valid, src, 0).astype(I32).reshape(N_BLOCKS * UNITS_PER_BLK)
    return e_b, n_used.astype(I32), n_valid, src


def _expert_kernel(be_ref, nu_ref, nv_ref, src_ref,
                   xs_hbm, wg_ref, wu_ref, wd_ref, ys_in_hbm, ys_hbm,
                   xbuf, ybuf, sem_in, sem_out):
    del be_ref, ys_in_hbm
    b = pl.program_id(0)
    n_used = nu_ref[0]
    slot = b % 2

    def in_copy(blk, m, sl):
        row = pl.multiple_of(src_ref[blk * UNITS_PER_BLK + m], ALIGN)
        return pltpu.make_async_copy(xs_hbm.at[pl.ds(row, ALIGN), :],
                                     xbuf.at[sl, pl.ds(m * ALIGN, ALIGN), :], sem_in.at[sl])

    def out_copy(blk, m, sl):
        row = pl.multiple_of(src_ref[blk * UNITS_PER_BLK + m], ALIGN)
        return pltpu.make_async_copy(ybuf.at[sl, pl.ds(m * ALIGN, ALIGN), :],
                                     ys_hbm.at[pl.ds(row, ALIGN), :], sem_out.at[sl])

    def for_units(blk, fn):
        for m in range(UNITS_PER_BLK):
            @pl.when(m < nv_ref[blk])
            def _():
                fn(m)

    @pl.when(b == 0)
    def _():
        xbuf[...] = jnp.zeros_like(xbuf)
        for_units(0, lambda m: in_copy(0, m, 0).start())

    @pl.when(b + 1 < n_used)
    def _():
        for_units(b + 1, lambda m: in_copy(b + 1, m, 1 - slot).start())

    @pl.when(b < n_used)
    def _():
        for_units(b, lambda m: in_copy(b, m, slot).wait())

        @pl.when(b >= 2)
        def _():
            for_units(b - 2, lambda m: out_copy(b - 2, m, slot).wait())

        xb = xbuf[slot]
        a = _dot(xb, wg_ref[0])
        u = _dot(xb, wu_ref[0])
        ybuf[slot] = _dot((jax.nn.silu(a) * u).astype(BF16), wd_ref[0]).astype(BF16)
        for_units(b, lambda m: out_copy(b, m, slot).start())

        @pl.when(b == n_used - 1)
        def _():
            @pl.when(b >= 1)
            def _():
                for_units(b - 1, lambda m: out_copy(b - 1, m, 1 - slot).wait())
            for_units(b, lambda m: out_copy(b, m, slot).wait())


def _expert_call(block_e, n_used, n_valid, src, xs, wg, wu, wd, ys):
    def w_map(i, be, nu, nv, sr):
        return (be[i], 0, 0)

    grid_spec = pltpu.PrefetchScalarGridSpec(
        num_scalar_prefetch=4,
        grid=(N_BLOCKS,),
        in_specs=[
            pl.BlockSpec(memory_space=pl.ANY),
            pl.BlockSpec((1, D_MODEL, D_EXPERT), w_map),
            pl.BlockSpec((1, D_MODEL, D_EXPERT), w_map),
            pl.BlockSpec((1, D_EXPERT, D_MODEL), w_map),
            pl.BlockSpec(memory_space=pl.ANY),
        ],
        out_specs=pl.BlockSpec(memory_space=pl.ANY),
        scratch_shapes=[
            pltpu.VMEM((2, EXP_BLK, D_MODEL), BF16),
            pltpu.VMEM((2, EXP_BLK, D_MODEL), BF16),
            pltpu.SemaphoreType.DMA((2,)),
            pltpu.SemaphoreType.DMA((2,)),
        ],
    )
    return pl.pallas_call(
        _expert_kernel,
        grid_spec=grid_spec,
        out_shape=jax.ShapeDtypeStruct((N_TILES * SORT_R, D_MODEL), BF16),
        input_output_aliases={8: 0},
        compiler_params=pltpu.CompilerParams(
            dimension_semantics=("arbitrary",), vmem_limit_bytes=VMEM_LIMIT, has_side_effects=True),
        name="moe_experts",
    )(block_e, n_used, n_valid, src, xs, wg, wu, wd, ys)


def kernel(x, c, w_ada, b_ada, norm1_g, norm2_g, w_in, w_ret_o, pool_w, pool_scale, w_pool_o, w_out,
           w_router, b_router, w_e_gate, w_e_up, w_e_down, final_g):
    assert x.shape == (BATCH, SEQ, D_MODEL) and w_in.shape == (DEPTH, D_MODEL, IN_W)
    xt = x.reshape(N_TOK, D_MODEL)
    mod = _ada_call(c, w_ada, b_ada).reshape(DEPTH, BATCH, 6, D_MODEL)

    rot = _rotary_tables()
    ret = _retention_tables()
    tri = (jnp.arange(TILE)[:, None] < jnp.arange(TILE)[None, :]).astype(BF16)
    wrt = w_router.T.astype(BF16)
    br = b_router.astype(F32).reshape(N_EXPERTS, 1)
    ys = jnp.zeros((N_TILES * SORT_R, D_MODEL), BF16)

    moe = None
    for l in range(DEPTH):
        g1 = norm1_g[l].reshape(1, D_MODEL)
        if moe is None:
            proj = _proj_call(xt, mod[l], g1, w_in[l].astype(BF16))
        else:
            xt, proj = _proj_call(xt, mod[l], g1, w_in[l].astype(BF16), moe)
        xt, xs, lw, n_pad = _mix_call(
            proj, xt, mod[l], norm2_g[l].reshape(1, D_MODEL), rot, ret,
            w_ret_o[l].astype(BF16), pool_w[l].astype(BF16), pool_scale[l].reshape(1, POOL_W),
            w_pool_o[l].astype(BF16), w_out[l].astype(BF16), wrt, br, tri)
        block_e, n_used, n_valid, src = _plan(n_pad[:, :, 0])
        ys = _expert_call(block_e, n_used, n_valid, src, xs.reshape(N_TILES * SORT_R, D_MODEL),
                          w_e_gate[l].astype(BF16), w_e_up[l].astype(BF16), w_e_down[l].astype(BF16), ys)
        moe = (lw.T, ys.reshape(N_TILES, SORT_R, D_MODEL), mod[l])
    out = _final_call(xt, *moe, final_g.reshape(1, D_MODEL))
    return out.reshape(BATCH, SEQ, D_MODEL)
```

```python
import functools

import numpy as np
import jax
import jax.numpy as jnp
from jax import lax
from jax.experimental import pallas as pl
from jax.experimental.pallas import tpu as pltpu

D_MODEL = 1024
BATCH = 16
SEQ = 2048
DEPTH = 4
N_TOK = BATCH * SEQ

RET_HEADS = 4
RET_QK_DIM = 128
RET_V_DIM = 256
RET_QK_W = RET_HEADS * RET_QK_DIM
RET_V_W = RET_HEADS * RET_V_DIM
ROPE_BASE = 10000.0
POOL_WINDOWS = (2, 4, 8, 16)
POOL_GROUPS = 4
POOL_GROUP_DIM = 128
POOL_W = POOL_GROUPS * POOL_GROUP_DIM
POOL_HALO = 16
IN_W = 2 * RET_QK_W + 2 * RET_V_W + POOL_W + 2 * D_MODEL
OFF_Q = 0
OFF_K = OFF_Q + RET_QK_W
OFF_V = OFF_K + RET_QK_W
OFF_G = OFF_V + RET_V_W
OFF_U = OFF_G + RET_V_W
OFF_AR = OFF_U + POOL_W
OFF_AP = OFF_AR + D_MODEL

N_EXPERTS = 16
N_GROUPS = 4
EXPERTS_PER_GROUP = 4
TOP_K = 2
D_EXPERT = D_MODEL // 2
EPS = 1e-6

TILE = 512
PROJ_TN = 512
RET_C = 256
N_TILES = N_TOK // TILE
ALIGN = 16
SORT_R = -(-(TOP_K * TILE + N_EXPERTS * (ALIGN - 1)) // 128) * 128
SORT_TN = 256
EXP_BLK = 256
UNITS_PER_BLK = EXP_BLK // ALIGN
MAX_UNITS = N_TILES * ((TOP_K * TILE + N_EXPERTS * (ALIGN - 1)) // ALIGN)
N_BLOCKS = -(-MAX_UNITS // UNITS_PER_BLK) + N_EXPERTS
DUMP_UNITS = N_EXPERTS * (UNITS_PER_BLK - 1)
YS_ROWS = (N_TILES + -(-DUMP_UNITS * ALIGN // SORT_R)) * SORT_R
VMEM_LIMIT = 56 * 1024 * 1024

F32 = jnp.float32
BF16 = jnp.bfloat16
I32 = jnp.int32


def _const_spec(shape):
    zeros = (0,) * len(shape)
    return pl.BlockSpec(shape, lambda *_: zeros, pipeline_mode=pl.Buffered(1))


def _rms(xf):
    return xf * lax.rsqrt(jnp.mean(xf * xf, axis=-1, keepdims=True) + EPS)


def _dot(a, b):
    return jnp.dot(a, b, preferred_element_type=F32)


def _ada_kernel(c_ref, w_ref, b_ref, o_ref):
    cond = jax.nn.silu(c_ref[...]).astype(BF16)
    o_ref[0] = _dot(cond, w_ref[0].astype(BF16)) + b_ref[0]


def _ada_call(c, w_ada, b_ada):
    n_col = 6
    return pl.pallas_call(
        _ada_kernel,
        grid=(DEPTH, n_col),
        in_specs=[
            pl.BlockSpec((BATCH, D_MODEL), lambda l, j: (0, 0)),
            pl.BlockSpec((1, D_MODEL, D_MODEL), lambda l, j: (l, 0, j)),
            pl.BlockSpec((1, 1, D_MODEL), lambda l, j: (l, 0, j)),
        ],
        out_specs=pl.BlockSpec((1, BATCH, D_MODEL), lambda l, j: (l, 0, j)),
        out_shape=jax.ShapeDtypeStruct((DEPTH, BATCH, n_col * D_MODEL), F32),
        compiler_params=pltpu.CompilerParams(
            dimension_semantics=("arbitrary", "arbitrary"), vmem_limit_bytes=VMEM_LIMIT),
        name="ada_mod",
    )(c, w_ada, b_ada.reshape(DEPTH, 1, n_col * D_MODEL))


def _moe_combine(x, cols, ys, gate):
    lane = lax.broadcasted_iota(I32, (TILE, SORT_R), 1)
    l0 = cols[:, 0:1].astype(I32)
    l1 = cols[:, 1:2].astype(I32)
    pw = jnp.where(lane == l0, cols[:, 2:3], 0.0) + jnp.where(lane == l1, cols[:, 3:4], 0.0)
    return x + gate * _dot(pw.astype(BF16), ys)


def _proj_kernel(combine, *refs):
    if combine:
        x_ref, cols_ref, ys_ref, modp_ref, mod_ref, g_ref, w_ref, xo_ref, o_ref = refs
        x = _moe_combine(x_ref[...], cols_ref[...], ys_ref[...], modp_ref[0, 5:6, :])
        xo_ref[...] = x
    else:
        x_ref, mod_ref, g_ref, w_ref, o_ref = refs
        x = x_ref[...]
    h = (_rms(x) * g_ref[...]) * (1.0 + mod_ref[0, 1:2, :]) + mod_ref[0, 0:1, :]
    hb = h.astype(BF16)
    for n in range(IN_W // PROJ_TN):
        cols = slice(n * PROJ_TN, (n + 1) * PROJ_TN)
        o_ref[:, cols] = _dot(hb, w_ref[:, cols]).astype(BF16)


def _proj_call(x, mod_l, g, w_in_l, moe=None):
    tiles_per_b = SEQ // TILE
    tok = pl.BlockSpec((TILE, D_MODEL), lambda i: (i, 0))
    mod_spec = pl.BlockSpec((1, 6, D_MODEL), lambda i: (i // tiles_per_b, 0, 0))
    proj_spec = pl.BlockSpec((TILE, IN_W), lambda i: (i, 0))
    proj_shape = jax.ShapeDtypeStruct((N_TOK, IN_W), BF16)
    tail_specs = [mod_spec, _const_spec((1, D_MODEL)), _const_spec((D_MODEL, IN_W))]
    if moe is None:
        in_specs = [tok] + tail_specs
        args = (x, mod_l, g, w_in_l)
        out_specs, out_shape = proj_spec, proj_shape
    else:
        cols, ys, mod_prev = moe
        in_specs = [tok, pl.BlockSpec((TILE, 4), lambda i: (i, 0)),
                    pl.BlockSpec((SORT_R, D_MODEL), lambda i: (i, 0)), mod_spec] + tail_specs
        args = (x, cols, ys, mod_prev, mod_l, g, w_in_l)
        out_specs = [tok, proj_spec]
        out_shape = [jax.ShapeDtypeStruct((N_TOK, D_MODEL), F32), proj_shape]
    return pl.pallas_call(
        functools.partial(_proj_kernel, moe is not None),
        grid=(N_TILES,),
        in_specs=in_specs,
        out_specs=out_specs,
        out_shape=out_shape,
        compiler_params=pltpu.CompilerParams(
            dimension_semantics=("arbitrary",), vmem_limit_bytes=VMEM_LIMIT),
        name="in_proj",
    )(*args)


def _final_kernel(x_ref, cols_ref, ys_ref, modp_ref, g_ref, o_ref):
    x = _moe_combine(x_ref[...], cols_ref[...], ys_ref[...], modp_ref[0, 5:6, :])
    o_ref[...] = _rms(x) * g_ref[...]


def _final_call(x, cols, ys, mod_prev, final_g):
    tiles_per_b = SEQ // TILE
    tok = pl.BlockSpec((TILE, D_MODEL), lambda i: (i, 0))
    return pl.pallas_call(
        _final_kernel,
        grid=(N_TILES,),
        in_specs=[tok, pl.BlockSpec((TILE, 4), lambda i: (i, 0)),
                  pl.BlockSpec((SORT_R, D_MODEL), lambda i: (i, 0)),
                  pl.BlockSpec((1, 6, D_MODEL), lambda i: (i // tiles_per_b, 0, 0)),
                  _const_spec((1, D_MODEL))],
        out_specs=tok,
        out_shape=jax.ShapeDtypeStruct((N_TOK, D_MODEL), F32),
        compiler_params=pltpu.CompilerParams(
            dimension_semantics=("arbitrary",), vmem_limit_bytes=VMEM_LIMIT),
        name="final_norm",
    )(x, cols, ys, mod_prev, final_g)


def _top2_sum(a, b, c, d):
    hi1, lo1 = jnp.maximum(a, b), jnp.minimum(a, b)
    hi2, lo2 = jnp.maximum(c, d), jnp.minimum(c, d)
    return jnp.maximum(hi1, hi2) + jnp.maximum(jnp.minimum(hi1, hi2), jnp.maximum(lo1, lo2))


def _first_argmax(vals):
    best = vals[0]
    idx = jnp.zeros(best.shape, I32)
    for j in range(1, len(vals)):
        better = vals[j] > best
        idx = jnp.where(better, j, idx)
        best = jnp.where(better, vals[j], best)
    return idx, best


def _mix_kernel(gamma_c,
                proj_ref, x_ref, mod_ref, n2g_ref, cq_ref, sq_ref, ck_ref, sk_ref,
                dec_ref, xi_ref, zeta_ref, wro_ref, pw_ref, ps_ref, wpo_ref, wout_ref,
                wrt_ref, br_ref, tri_ref,
                xo_ref, xs_ref, lw_ref, np_ref,
                state_ref, halo_ref, gated_ref):
    s = pl.program_id(1)
    ts = TILE

    @pl.when(s == 0)
    def _():
        state_ref[...] = jnp.zeros_like(state_ref)
        halo_ref[...] = jnp.zeros_like(halo_ref)

    for j in range(ts // RET_C):
        rows = slice(j * RET_C, (j + 1) * RET_C)
        cq, sq = cq_ref[rows, :], sq_ref[rows, :]
        ck, sk = ck_ref[rows, :], sk_ref[rows, :]
        for h in range(RET_HEADS):
            qh = proj_ref[rows, OFF_Q + h * RET_QK_DIM:OFF_Q + (h + 1) * RET_QK_DIM].astype(F32)
            kh = proj_ref[rows, OFF_K + h * RET_QK_DIM:OFF_K + (h + 1) * RET_QK_DIM].astype(F32)
            vh = proj_ref[rows, OFF_V + h * RET_V_DIM:OFF_V + (h + 1) * RET_V_DIM]
            qh = qh * cq + pltpu.roll(qh, RET_QK_DIM // 2, axis=1) * sq
            kh = kh * ck + pltpu.roll(kh, RET_QK_DIM // 2, axis=1) * sk
            sc = lax.dot_general(qh.astype(BF16), kh.astype(BF16), (((1,), (1,)), ((), ())),
                                 preferred_element_type=F32) * dec_ref[h]
            st = state_ref[h]
            y = _dot(sc.astype(BF16), vh) + _dot((qh * xi_ref[h]).astype(BF16), st.astype(BF16))
            kv = lax.dot_general((kh * zeta_ref[h]).astype(BF16), vh, (((0,), (0,)), ((), ())),
                                 preferred_element_type=F32)
            state_ref[h] = st * gamma_c[h] + kv
            gh = proj_ref[rows, OFF_G + h * RET_V_DIM:OFF_G + (h + 1) * RET_V_DIM].astype(F32)
            gated_ref[rows, h * RET_V_DIM:(h + 1) * RET_V_DIM] = (jax.nn.silu(gh) * _rms(y)).astype(BF16)
    y_ret = _dot(gated_ref[...], wro_ref[...])

    u = proj_ref[:, OFF_U:OFF_U + POOL_W].astype(F32)
    ext = jnp.concatenate([halo_ref[...], u], axis=0)
    halo_ref[...] = u[ts - POOL_HALO:, :]
    pos = s * ts + lax.broadcasted_iota(I32, (ts, 1), 0)
    pooled = []
    for g, win in enumerate(POOL_WINDOWS):
        cols = slice(g * POOL_GROUP_DIM, (g + 1) * POOL_GROUP_DIM)
        acc = ext[:, cols]
        shift = 1
        while shift < win:
            acc = acc + pltpu.roll(acc, shift, axis=0)
            shift *= 2
        count = jnp.minimum(pos + 1, win).astype(F32)
        mix = acc[POOL_HALO:, :] / count - u[:, cols]
        pooled.append(_dot(mix.astype(BF16), pw_ref[g]))
    y_pool = _dot((jnp.concatenate(pooled, axis=1) * ps_ref[...]).astype(BF16), wpo_ref[...])

    a_ret = proj_ref[:, OFF_AR:OFF_AR + D_MODEL].astype(F32)
    a_pool = proj_ref[:, OFF_AP:OFF_AP + D_MODEL].astype(F32)
    merged = jax.nn.sigmoid(a_ret) * y_ret + jax.nn.sigmoid(a_pool) * y_pool
    x_new = x_ref[...] + mod_ref[0, 2:3, :] * _dot(merged.astype(BF16), wout_ref[...])
    xo_ref[...] = x_new

    h2 = (_rms(x_new) * n2g_ref[...]) * (1.0 + mod_ref[0, 4:5, :]) + mod_ref[0, 3:4, :]
    h2b = h2.astype(BF16)

    logits = lax.dot_general(wrt_ref[...], h2b, (((1,), (1,)), ((), ())),
                             preferred_element_type=F32) + br_ref[...]
    p = jnp.exp(logits - jnp.max(logits, axis=0, keepdims=True))
    probs = p / jnp.sum(p, axis=0, keepdims=True)
    pr = [probs[e:e + 1, :] for e in range(N_EXPERTS)]
    scores = [_top2_sum(*pr[g * EXPERTS_PER_GROUP:(g + 1) * EXPERTS_PER_GROUP]) for g in range(N_GROUPS)]
    grp, _ = _first_argmax(scores)
    in_grp = []
    for jj in range(EXPERTS_PER_GROUP):
        v = pr[jj]
        for g in range(1, N_GROUPS):
            v = jnp.where(grp == g, pr[g * EXPERTS_PER_GROUP + jj], v)
        in_grp.append(v)
    i0, p0 = _first_argmax(in_grp)
    i1, p1 = _first_argmax([jnp.where(i0 == jj, -1.0, in_grp[jj]) for jj in range(EXPERTS_PER_GROUP)])
    e0 = grp * EXPERTS_PER_GROUP + i0
    e1 = grp * EXPERTS_PER_GROUP + i1
    den = p0 + p1

    erow = lax.broadcasted_iota(I32, (N_EXPERTS, ts), 0)
    oh0 = erow == e0
    oh1 = erow == e1
    ohf = jnp.where(oh0 | oh1, 1.0, 0.0)
    n_e = jnp.sum(ohf, axis=1, keepdims=True)
    n_pad = jnp.broadcast_to(jnp.floor((n_e + (ALIGN - 1.0)) * (1.0 / ALIGN)) * ALIGN, (N_EXPERTS, 128))
    np_ref[0] = n_pad
    erow128 = lax.broadcasted_iota(I32, (N_EXPERTS, 128), 0)
    run_end = n_pad
    shift = 1
    while shift < N_EXPERTS:
        run_end = run_end + jnp.where(erow128 >= shift, pltpu.roll(run_end, shift, axis=0), 0.0)
        shift *= 2
    run_start = (run_end - n_pad)[:, 0:1]
    row_of = _dot(ohf.astype(BF16), tri_ref[...]) + run_start
    l0 = jnp.sum(jnp.where(oh0, row_of, 0.0), axis=0, keepdims=True)
    l1 = jnp.sum(jnp.where(oh1, row_of, 0.0), axis=0, keepdims=True)
    lw_ref[...] = jnp.concatenate([l0, l1, p0 / den, p1 / den], axis=0)

    srow = lax.broadcasted_iota(I32, (SORT_R, ts), 0)
    perm = jnp.where((srow == l0.astype(I32)) | (srow == l1.astype(I32)), 1.0, 0.0).astype(BF16)
    for n in range(D_MODEL // SORT_TN):
        cols = slice(n * SORT_TN, (n + 1) * SORT_TN)
        xs_ref[0, :, cols] = _dot(perm, h2b[:, cols]).astype(BF16)


def _retention_tables():
    lg = np.log(1.0 - 2.0 ** (-5.0 - np.arange(RET_HEADS, dtype=np.float32))).astype(np.float32)
    idx = np.arange(RET_C, dtype=np.float32)
    rel = idx[:, None] - idx[None, :]
    causal = rel >= 0
    dec = np.where(causal[None], np.exp(lg[:, None, None] * np.where(causal, rel, 0.0)[None]), 0.0)
    xi = np.exp(lg[:, None] * (idx[None, :] + 1.0))
    zeta = np.exp(lg[:, None] * (RET_C - 1.0 - idx[None, :]))
    gamma_c = tuple(float(v) for v in np.exp(lg * RET_C).astype(np.float32))
    xi = np.broadcast_to(xi[:, :, None], (RET_HEADS, RET_C, RET_QK_DIM))
    zeta = np.broadcast_to(zeta[:, :, None], (RET_HEADS, RET_C, RET_QK_DIM))
    return (jnp.asarray(dec, F32), jnp.asarray(xi, F32), jnp.asarray(zeta, F32), gamma_c)


def _rotary_tables():
    freqs = ROPE_BASE ** (-jnp.arange(0, RET_QK_DIM, 2, dtype=F32) / RET_QK_DIM)
    ang = jnp.arange(SEQ, dtype=F32)[:, None] * freqs[None, :]
    cos, sin = jnp.cos(ang), jnp.sin(ang)
    cos2 = jnp.concatenate([cos, cos], axis=1)
    sin2 = jnp.concatenate([-sin, sin], axis=1)
    k_scale = RET_QK_DIM ** -0.5
    return cos2, sin2, cos2 * k_scale, sin2 * k_scale


def _mix_call(proj, x, mod_l, n2g, rot, ret, wro, pw, ps, wpo, wout, wrt, br, tri):
    dec, xi, zeta, gamma_c = ret
    ts = TILE
    n_s = SEQ // ts
    tok = lambda b, s: (b * n_s + s, 0)
    seq_tab = pl.BlockSpec((ts, RET_QK_DIM), lambda b, s: (s, 0))
    return pl.pallas_call(
        functools.partial(_mix_kernel, gamma_c),
        grid=(BATCH, n_s),
        in_specs=[
            pl.BlockSpec((ts, IN_W), tok),
            pl.BlockSpec((ts, D_MODEL), tok),
            pl.BlockSpec((1, 6, D_MODEL), lambda b, s: (b, 0, 0)),
            _const_spec((1, D_MODEL)),
            seq_tab, seq_tab, seq_tab, seq_tab,
            _const_spec((RET_HEADS, RET_C, RET_C)),
            _const_spec((RET_HEADS, RET_C, RET_QK_DIM)),
            _const_spec((RET_HEADS, RET_C, RET_QK_DIM)),
            _const_spec((RET_V_W, D_MODEL)),
            _const_spec((POOL_GROUPS, POOL_GROUP_DIM, POOL_GROUP_DIM)),
            _const_spec((1, POOL_W)),
            _const_spec((POOL_W, D_MODEL)),
            _const_spec((D_MODEL, D_MODEL)),
            _const_spec((N_EXPERTS, D_MODEL)),
            _const_spec((N_EXPERTS, 1)),
            _const_spec((ts, ts)),
        ],
        out_specs=[
            pl.BlockSpec((ts, D_MODEL), tok),
            pl.BlockSpec((1, SORT_R, D_MODEL), lambda b, s: (b * n_s + s, 0, 0)),
            pl.BlockSpec((4, ts), lambda b, s: (0, b * n_s + s)),
            pl.BlockSpec((1, N_EXPERTS, 128), lambda b, s: (b * n_s + s, 0, 0)),
        ],
        out_shape=[
            jax.ShapeDtypeStruct((N_TOK, D_MODEL), F32),
            jax.ShapeDtypeStruct((N_TILES, SORT_R, D_MODEL), BF16),
            jax.ShapeDtypeStruct((4, N_TOK), F32),
            jax.ShapeDtypeStruct((N_TILES, N_EXPERTS, 128), F32),
        ],
        scratch_shapes=[
            pltpu.VMEM((RET_HEADS, RET_QK_DIM, RET_V_DIM), F32),
            pltpu.VMEM((POOL_HALO, POOL_W), F32),
            pltpu.VMEM((ts, RET_V_W), BF16),
        ],
        compiler_params=pltpu.CompilerParams(
            dimension_semantics=("arbitrary", "arbitrary"), vmem_limit_bytes=VMEM_LIMIT),
        name="mixer_route",
    )(proj, x, mod_l, n2g, *rot, dec, xi, zeta, wro, pw, ps, wpo, wout, wrt, br, tri)


def _plan(n_pad):
    n_pad = n_pad.astype(I32)
    units = (n_pad // ALIGN).T
    run_row = (jnp.cumsum(n_pad, axis=1) - n_pad).T
    unit_start = jnp.cumsum(units, axis=1) - units
    total = units.sum(axis=1)
    n_blk = (total + UNITS_PER_BLK - 1) // UNITS_PER_BLK
    blk_end = jnp.cumsum(n_blk)
    blk_start = blk_end - n_blk
    n_used = blk_end[-1:]
    b = jnp.arange(N_BLOCKS + 1, dtype=I32)
    e_b = jnp.minimum((b[:, None] >= blk_end[None, :]).sum(axis=1), N_EXPERTS - 1).astype(I32)
    oh_e = (e_b[:, None] == jnp.arange(N_EXPERTS, dtype=I32)[None, :]).astype(I32)
    pick = lambda tab: (oh_e[:, :, None] * tab[None, :, :]).sum(axis=1)
    unit_start_b, units_b, run_row_b = pick(unit_start), pick(units), pick(run_row)
    total_b = (oh_e * total[None, :]).sum(axis=1)
    first_q = (b - (oh_e * blk_start[None, :]).sum(axis=1)) * UNITS_PER_BLK
    n_valid = jnp.where(b < n_used[0], jnp.clip(total_b - first_q, 0, UNITS_PER_BLK), 0).astype(I32)
    q = first_q[:, None] + jnp.arange(UNITS_PER_BLK, dtype=I32)[None, :]
    tile = jnp.minimum(((unit_start_b + units_b)[:, None, :] <= q[:, :, None]).sum(axis=2), N_TILES - 1)
    oh_t = (tile[:, :, None] == jnp.arange(N_TILES, dtype=I32)[None, None, :]).astype(I32)
    base = jnp.arange(N_TILES, dtype=I32)[None, :] * SORT_R + run_row_b - unit_start_b * ALIGN
    row = q * ALIGN + (oh_t * base[:, None, :]).sum(axis=2)
    m = jnp.arange(UNITS_PER_BLK, dtype=I32)[None, :]
    valid = m < n_valid[:, None]
    pad_rank = jnp.clip(m - n_valid[:, None], 0, UNITS_PER_BLK - 2)
    dump = N_TILES * SORT_R + (e_b[:, None] * (UNITS_PER_BLK - 1) + pad_rank) * ALIGN
    src = jnp.where(valid, row, 0).astype(I32).reshape(-1)
    dst = jnp.where(valid, row, dump).astype(I32).reshape(-1)
    return e_b, n_used.astype(I32), src, dst


def _expert_kernel(be_ref, nu_ref, src_ref, dst_ref,
                   xs_hbm, wg_ref, wu_ref, wd_ref, ys_in_hbm, ys_hbm,
                   xbuf, ybuf, sem_in, sem_out):
    del be_ref, ys_in_hbm
    b = pl.program_id(0)
    n_used = nu_ref[0]
    slot = b % 2

    def start_in(blk, sl):
        for m in range(UNITS_PER_BLK):
            row = pl.multiple_of(src_ref[blk * UNITS_PER_BLK + m], ALIGN)
            pltpu.make_async_copy(xs_hbm.at[pl.ds(row, ALIGN), :],
                                  xbuf.at[sl, pl.ds(m * ALIGN, ALIGN), :], sem_in.at[sl]).start()

    def start_out(blk, sl):
        for m in range(UNITS_PER_BLK):
            row = pl.multiple_of(dst_ref[blk * UNITS_PER_BLK + m], ALIGN)
            pltpu.make_async_copy(ybuf.at[sl, pl.ds(m * ALIGN, ALIGN), :],
                                  ys_hbm.at[pl.ds(row, ALIGN), :], sem_out.at[sl]).start()

    def wait_in(sl):
        pltpu.make_async_copy(xs_hbm.at[pl.ds(0, EXP_BLK), :], xbuf.at[sl], sem_in.at[sl]).wait()

    def wait_out(sl):
        pltpu.make_async_copy(ybuf.at[sl], ys_hbm.at[pl.ds(0, EXP_BLK), :], sem_out.at[sl]).wait()

    @pl.when(b == 0)
    def _():
        start_in(0, 0)

    @pl.when(b < n_used)
    def _():
        wait_in(slot)

        @pl.when(b >= 2)
        def _():
            wait_out(slot)

        start_in(b + 1, 1 - slot)
        xb = xbuf[slot]
        a = _dot(xb, wg_ref[0])
        u = _dot(xb, wu_ref[0])
        ybuf[slot] = _dot((jax.nn.silu(a) * u).astype(BF16), wd_ref[0]).astype(BF16)
        start_out(b, slot)

        @pl.when(b == n_used - 1)
        def _():
            wait_in(1 - slot)

            @pl.when(b >= 1)
            def _():
                wait_out(1 - slot)
            wait_out(slot)


def _expert_call(block_e, n_used, src, dst, xs, wg, wu, wd, ys):
    def w_map(i, be, nu, sr, ds):
        return (be[i], 0, 0)

    grid_spec = pltpu.PrefetchScalarGridSpec(
        num_scalar_prefetch=4,
        grid=(N_BLOCKS,),
        in_specs=[
            pl.BlockSpec(memory_space=pl.ANY),
            pl.BlockSpec((1, D_MODEL, D_EXPERT), w_map),
            pl.BlockSpec((1, D_MODEL, D_EXPERT), w_map),
            pl.BlockSpec((1, D_EXPERT, D_MODEL), w_map),
            pl.BlockSpec(memory_space=pl.ANY),
        ],
        out_specs=pl.BlockSpec(memory_space=pl.ANY),
        scratch_shapes=[
            pltpu.VMEM((2, EXP_BLK, D_MODEL), BF16),
            pltpu.VMEM((2, EXP_BLK, D_MODEL), BF16),
            pltpu.SemaphoreType.DMA((2,)),
            pltpu.SemaphoreType.DMA((2,)),
        ],
    )
    return pl.pallas_call(
        _expert_kernel,
        grid_spec=grid_spec,
        out_shape=jax.ShapeDtypeStruct((YS_ROWS, D_MODEL), BF16),
        input_output_aliases={8: 0},
        compiler_params=pltpu.CompilerParams(
            dimension_semantics=("arbitrary",), vmem_limit_bytes=VMEM_LIMIT, has_side_effects=True),
        name="moe_experts",
    )(block_e, n_used, src, dst, xs, wg, wu, wd, ys)


def kernel(x, c, w_ada, b_ada, norm1_g, norm2_g, w_in, w_ret_o, pool_w, pool_scale, w_pool_o, w_out,
           w_router, b_router, w_e_gate, w_e_up, w_e_down, final_g):
    assert x.shape == (BATCH, SEQ, D_MODEL) and w_in.shape == (DEPTH, D_MODEL, IN_W)
    xt = x.reshape(N_TOK, D_MODEL)
    mod = _ada_call(c, w_ada, b_ada).reshape(DEPTH, BATCH, 6, D_MODEL)

    rot = _rotary_tables()
    ret = _retention_tables()
    tri = (jnp.arange(TILE)[:, None] < jnp.arange(TILE)[None, :]).astype(BF16)
    wrt = w_router.T.astype(BF16)
    br = b_router.astype(F32).reshape(N_EXPERTS, 1)
    ys = jnp.zeros((YS_ROWS, D_MODEL), BF16)

    moe = None
    for l in range(DEPTH):
        g1 = norm1_g[l].reshape(1, D_MODEL)
        if moe is None:
            proj = _proj_call(xt, mod[l], g1, w_in[l].astype(BF16))
        else:
            xt, proj = _proj_call(xt, mod[l], g1, w_in[l].astype(BF16), moe)
        xt, xs, lw, n_pad = _mix_call(
            proj, xt, mod[l], norm2_g[l].reshape(1, D_MODEL), rot, ret,
            w_ret_o[l].astype(BF16), pool_w[l].astype(BF16), pool_scale[l].reshape(1, POOL_W),
            w_pool_o[l].astype(BF16), w_out[l].astype(BF16), wrt, br, tri)
        block_e, n_used, src, dst = _plan(n_pad[:, :, 0])
        ys = _expert_call(block_e, n_used, src, dst, xs.reshape(N_TILES * SORT_R, D_MODEL),
                          w_e_gate[l].astype(BF16), w_e_up[l].astype(BF16), w_e_down[l].astype(BF16), ys)
        moe = (lw.T, ys, mod[l])
    out = _final_call(xt, *moe, final_g.reshape(1, D_MODEL))
    return out.reshape(BATCH, SEQ, D_MODEL)
```

```python
import functools

import numpy as np
import jax
import jax.numpy as jnp
from jax import lax
from jax.experimental import pallas as pl
from jax.experimental.pallas import tpu as pltpu

D_MODEL = 1024
BATCH = 16
SEQ = 2048
DEPTH = 4
N_TOK = BATCH * SEQ

RET_HEADS = 4
RET_QK_DIM = 128
RET_V_DIM = 256
RET_QK_W = RET_HEADS * RET_QK_DIM
RET_V_W = RET_HEADS * RET_V_DIM
ROPE_BASE = 10000.0
POOL_WINDOWS = (2, 4, 8, 16)
POOL_GROUPS = 4
POOL_GROUP_DIM = 128
POOL_W = POOL_GROUPS * POOL_GROUP_DIM
POOL_HALO = 16
IN_W = 2 * RET_QK_W + 2 * RET_V_W + POOL_W + 2 * D_MODEL
OFF_Q = 0
OFF_K = OFF_Q + RET_QK_W
OFF_V = OFF_K + RET_QK_W
OFF_G = OFF_V + RET_V_W
OFF_U = OFF_G + RET_V_W
OFF_AR = OFF_U + POOL_W
OFF_AP = OFF_AR + D_MODEL

N_EXPERTS = 16
N_GROUPS = 4
EXPERTS_PER_GROUP = 4
TOP_K = 2
D_EXPERT = D_MODEL // 2
EPS = 1e-6

TILE = 512
PROJ_TN = 512
RET_C = 256
N_TILES = N_TOK // TILE
ALIGN = 16
SORT_R = -(-(TOP_K * TILE + N_EXPERTS * (ALIGN - 1)) // 128) * 128
SORT_TN = 256
EXP_BLK = 512
UNITS_PER_BLK = EXP_BLK // ALIGN
MAX_UNITS = N_TILES * ((TOP_K * TILE + N_EXPERTS * (ALIGN - 1)) // ALIGN)
N_BLOCKS = -(-MAX_UNITS // UNITS_PER_BLK) + N_EXPERTS
DUMP_UNITS = N_EXPERTS * (UNITS_PER_BLK - 1)
YS_ROWS = (N_TILES + -(-DUMP_UNITS * ALIGN // SORT_R)) * SORT_R
VMEM_LIMIT = 56 * 1024 * 1024

F32 = jnp.float32
BF16 = jnp.bfloat16
I32 = jnp.int32


def _const_spec(shape):
    zeros = (0,) * len(shape)
    return pl.BlockSpec(shape, lambda *_: zeros, pipeline_mode=pl.Buffered(1))


def _rms(xf):
    return xf * lax.rsqrt(jnp.mean(xf * xf, axis=-1, keepdims=True) + EPS)


def _dot(a, b):
    return jnp.dot(a, b, preferred_element_type=F32)


def _ada_kernel(c_ref, w_ref, b_ref, o_ref):
    cond = jax.nn.silu(c_ref[...]).astype(BF16)
    o_ref[0] = _dot(cond, w_ref[0].astype(BF16)) + b_ref[0]


def _ada_call(c, w_ada, b_ada):
    n_col = 6
    return pl.pallas_call(
        _ada_kernel,
        grid=(DEPTH, n_col),
        in_specs=[
            pl.BlockSpec((BATCH, D_MODEL), lambda l, j: (0, 0)),
            pl.BlockSpec((1, D_MODEL, D_MODEL), lambda l, j: (l, 0, j)),
            pl.BlockSpec((1, 1, D_MODEL), lambda l, j: (l, 0, j)),
        ],
        out_specs=pl.BlockSpec((1, BATCH, D_MODEL), lambda l, j: (l, 0, j)),
        out_shape=jax.ShapeDtypeStruct((DEPTH, BATCH, n_col * D_MODEL), F32),
        compiler_params=pltpu.CompilerParams(
            dimension_semantics=("arbitrary", "arbitrary"), vmem_limit_bytes=VMEM_LIMIT),
        name="ada_mod",
    )(c, w_ada, b_ada.reshape(DEPTH, 1, n_col * D_MODEL))


def _moe_combine(x, cols, ys, gate):
    lane = lax.broadcasted_iota(I32, (TILE, SORT_R), 1)
    l0 = cols[:, 0:1].astype(I32)
    l1 = cols[:, 1:2].astype(I32)
    pw = jnp.where(lane == l0, cols[:, 2:3], 0.0) + jnp.where(lane == l1, cols[:, 3:4], 0.0)
    return x + gate * _dot(pw.astype(BF16), ys)


def _proj_kernel(combine, *refs):
    if combine:
        x_ref, cols_ref, ys_ref, modp_ref, mod_ref, g_ref, w_ref, xo_ref, o_ref = refs
        x = _moe_combine(x_ref[...], cols_ref[...], ys_ref[...], modp_ref[0, 5:6, :])
        xo_ref[...] = x
    else:
        x_ref, mod_ref, g_ref, w_ref, o_ref = refs
        x = x_ref[...]
    h = (_rms(x) * g_ref[...]) * (1.0 + mod_ref[0, 1:2, :]) + mod_ref[0, 0:1, :]
    hb = h.astype(BF16)
    for n in range(IN_W // PROJ_TN):
        cols = slice(n * PROJ_TN, (n + 1) * PROJ_TN)
        o_ref[:, cols] = _dot(hb, w_ref[:, cols]).astype(BF16)


def _proj_call(x, mod_l, g, w_in_l, moe=None):
    tiles_per_b = SEQ // TILE
    tok = pl.BlockSpec((TILE, D_MODEL), lambda i: (i, 0))
    mod_spec = pl.BlockSpec((1, 6, D_MODEL), lambda i: (i // tiles_per_b, 0, 0))
    proj_spec = pl.BlockSpec((TILE, IN_W), lambda i: (i, 0))
    proj_shape = jax.ShapeDtypeStruct((N_TOK, IN_W), BF16)
    tail_specs = [mod_spec, _const_spec((1, D_MODEL)), _const_spec((D_MODEL, IN_W))]
    if moe is None:
        in_specs = [tok] + tail_specs
        args = (x, mod_l, g, w_in_l)
        out_specs, out_shape = proj_spec, proj_shape
    else:
        cols, ys, mod_prev = moe
        in_specs = [tok, pl.BlockSpec((TILE, 4), lambda i: (i, 0)),
                    pl.BlockSpec((SORT_R, D_MODEL), lambda i: (i, 0)), mod_spec] + tail_specs
        args = (x, cols, ys, mod_prev, mod_l, g, w_in_l)
        out_specs = [tok, proj_spec]
        out_shape = [jax.ShapeDtypeStruct((N_TOK, D_MODEL), F32), proj_shape]
    return pl.pallas_call(
        functools.partial(_proj_kernel, moe is not None),
        grid=(N_TILES,),
        in_specs=in_specs,
        out_specs=out_specs,
        out_shape=out_shape,
        compiler_params=pltpu.CompilerParams(
            dimension_semantics=("arbitrary",), vmem_limit_bytes=VMEM_LIMIT),
        name="in_proj",
    )(*args)


def _final_kernel(x_ref, cols_ref, ys_ref, modp_ref, g_ref, o_ref):
    x = _moe_combine(x_ref[...], cols_ref[...], ys_ref[...], modp_ref[0, 5:6, :])
    o_ref[...] = _rms(x) * g_ref[...]


def _final_call(x, cols, ys, mod_prev, final_g):
    tiles_per_b = SEQ // TILE
    tok = pl.BlockSpec((TILE, D_MODEL), lambda i: (i, 0))
    return pl.pallas_call(
        _final_kernel,
        grid=(N_TILES,),
        in_specs=[tok, pl.BlockSpec((TILE, 4), lambda i: (i, 0)),
                  pl.BlockSpec((SORT_R, D_MODEL), lambda i: (i, 0)),
                  pl.BlockSpec((1, 6, D_MODEL), lambda i: (i // tiles_per_b, 0, 0)),
                  _const_spec((1, D_MODEL))],
        out_specs=tok,
        out_shape=jax.ShapeDtypeStruct((N_TOK, D_MODEL), F32),
        compiler_params=pltpu.CompilerParams(
            dimension_semantics=("arbitrary",), vmem_limit_bytes=VMEM_LIMIT),
        name="final_norm",
    )(x, cols, ys, mod_prev, final_g)


def _top2_sum(a, b, c, d):
    hi1, lo1 = jnp.maximum(a, b), jnp.minimum(a, b)
    hi2, lo2 = jnp.maximum(c, d), jnp.minimum(c, d)
    return jnp.maximum(hi1, hi2) + jnp.maximum(jnp.minimum(hi1, hi2), jnp.maximum(lo1, lo2))


def _first_argmax(vals):
    best = vals[0]
    idx = jnp.zeros(best.shape, I32)
    for j in range(1, len(vals)):
        better = vals[j] > best
        idx = jnp.where(better, j, idx)
        best = jnp.where(better, vals[j], best)
    return idx, best


def _mix_kernel(gamma_c,
                proj_ref, x_ref, mod_ref, n2g_ref, cq_ref, sq_ref, ck_ref, sk_ref,
                dec_ref, xi_ref, zeta_ref, wro_ref, pw_ref, ps_ref, wpo_ref, wout_ref,
                wrt_ref, br_ref, tri_ref,
                xo_ref, xs_ref, lw_ref, np_ref,
                state_ref, halo_ref, gated_ref):
    s = pl.program_id(1)
    ts = TILE

    @pl.when(s == 0)
    def _():
        state_ref[...] = jnp.zeros_like(state_ref)
        halo_ref[...] = jnp.zeros_like(halo_ref)

    for j in range(ts // RET_C):
        rows = slice(j * RET_C, (j + 1) * RET_C)
        cq, sq = cq_ref[rows, :], sq_ref[rows, :]
        ck, sk = ck_ref[rows, :], sk_ref[rows, :]
        for h in range(RET_HEADS):
            qh = proj_ref[rows, OFF_Q + h * RET_QK_DIM:OFF_Q + (h + 1) * RET_QK_DIM].astype(F32)
            kh = proj_ref[rows, OFF_K + h * RET_QK_DIM:OFF_K + (h + 1) * RET_QK_DIM].astype(F32)
            vh = proj_ref[rows, OFF_V + h * RET_V_DIM:OFF_V + (h + 1) * RET_V_DIM]
            qh = qh * cq + pltpu.roll(qh, RET_QK_DIM // 2, axis=1) * sq
            kh = kh * ck + pltpu.roll(kh, RET_QK_DIM // 2, axis=1) * sk
            sc = lax.dot_general(qh.astype(BF16), kh.astype(BF16), (((1,), (1,)), ((), ())),
                                 preferred_element_type=F32) * dec_ref[h]
            st = state_ref[h]
            y = _dot(sc.astype(BF16), vh) + _dot((qh * xi_ref[h]).astype(BF16), st.astype(BF16))
            kv = lax.dot_general((kh * zeta_ref[h]).astype(BF16), vh, (((0,), (0,)), ((), ())),
                                 preferred_element_type=F32)
            state_ref[h] = st * gamma_c[h] + kv
            gh = proj_ref[rows, OFF_G + h * RET_V_DIM:OFF_G + (h + 1) * RET_V_DIM].astype(F32)
            gated_ref[rows, h * RET_V_DIM:(h + 1) * RET_V_DIM] = (jax.nn.silu(gh) * _rms(y)).astype(BF16)
    y_ret = _dot(gated_ref[...], wro_ref[...])

    u = proj_ref[:, OFF_U:OFF_U + POOL_W].astype(F32)
    ext = jnp.concatenate([halo_ref[...], u], axis=0)
    halo_ref[...] = u[ts - POOL_HALO:, :]
    pos = s * ts + lax.broadcasted_iota(I32, (ts, 1), 0)
    pooled = []
    for g, win in enumerate(POOL_WINDOWS):
        cols = slice(g * POOL_GROUP_DIM, (g + 1) * POOL_GROUP_DIM)
        acc = ext[:, cols]
        shift = 1
        while shift < win:
            acc = acc + pltpu.roll(acc, shift, axis=0)
            shift *= 2
        count = jnp.minimum(pos + 1, win).astype(F32)
        mix = acc[POOL_HALO:, :] / count - u[:, cols]
        pooled.append(_dot(mix.astype(BF16), pw_ref[g]))
    y_pool = _dot((jnp.concatenate(pooled, axis=1) * ps_ref[...]).astype(BF16), wpo_ref[...])

    a_ret = proj_ref[:, OFF_AR:OFF_AR + D_MODEL].astype(F32)
    a_pool = proj_ref[:, OFF_AP:OFF_AP + D_MODEL].astype(F32)
    merged = jax.nn.sigmoid(a_ret) * y_ret + jax.nn.sigmoid(a_pool) * y_pool
    x_new = x_ref[...] + mod_ref[0, 2:3, :] * _dot(merged.astype(BF16), wout_ref[...])
    xo_ref[...] = x_new

    h2 = (_rms(x_new) * n2g_ref[...]) * (1.0 + mod_ref[0, 4:5, :]) + mod_ref[0, 3:4, :]
    h2b = h2.astype(BF16)

    logits = lax.dot_general(wrt_ref[...], h2b, (((1,), (1,)), ((), ())),
                             preferred_element_type=F32) + br_ref[...]
    p = jnp.exp(logits - jnp.max(logits, axis=0, keepdims=True))
    probs = p / jnp.sum(p, axis=0, keepdims=True)
    pr = [probs[e:e + 1, :] for e in range(N_EXPERTS)]
    scores = [_top2_sum(*pr[g * EXPERTS_PER_GROUP:(g + 1) * EXPERTS_PER_GROUP]) for g in range(N_GROUPS)]
    grp, _ = _first_argmax(scores)
    in_grp = []
    for jj in range(EXPERTS_PER_GROUP):
        v = pr[jj]
        for g in range(1, N_GROUPS):
            v = jnp.where(grp == g, pr[g * EXPERTS_PER_GROUP + jj], v)
        in_grp.append(v)
    i0, p0 = _first_argmax(in_grp)
    i1, p1 = _first_argmax([jnp.where(i0 == jj, -1.0, in_grp[jj]) for jj in range(EXPERTS_PER_GROUP)])
    e0 = grp * EXPERTS_PER_GROUP + i0
    e1 = grp * EXPERTS_PER_GROUP + i1
    den = p0 + p1

    erow = lax.broadcasted_iota(I32, (N_EXPERTS, ts), 0)
    oh0 = erow == e0
    oh1 = erow == e1
    ohf = jnp.where(oh0 | oh1, 1.0, 0.0)
    n_e = jnp.sum(ohf, axis=1, keepdims=True)
    n_pad = jnp.broadcast_to(jnp.floor((n_e + (ALIGN - 1.0)) * (1.0 / ALIGN)) * ALIGN, (N_EXPERTS, 128))
    np_ref[0] = n_pad
    erow128 = lax.broadcasted_iota(I32, (N_EXPERTS, 128), 0)
    run_end = n_pad
    shift = 1
    while shift < N_EXPERTS:
        run_end = run_end + jnp.where(erow128 >= shift, pltpu.roll(run_end, shift, axis=0), 0.0)
        shift *= 2
    run_start = (run_end - n_pad)[:, 0:1]
    row_of = _dot(ohf.astype(BF16), tri_ref[...]) + run_start
    l0 = jnp.sum(jnp.where(oh0, row_of, 0.0), axis=0, keepdims=True)
    l1 = jnp.sum(jnp.where(oh1, row_of, 0.0), axis=0, keepdims=True)
    lw_ref[...] = jnp.concatenate([l0, l1, p0 / den, p1 / den], axis=0)

    srow = lax.broadcasted_iota(I32, (SORT_R, ts), 0)
    perm = jnp.where((srow == l0.astype(I32)) | (srow == l1.astype(I32)), 1.0, 0.0).astype(BF16)
    for n in range(D_MODEL // SORT_TN):
        cols = slice(n * SORT_TN, (n + 1) * SORT_TN)
        xs_ref[0, :, cols] = _dot(perm, h2b[:, cols]).astype(BF16)


def _retention_tables():
    lg = np.log(1.0 - 2.0 ** (-5.0 - np.arange(RET_HEADS, dtype=np.float32))).astype(np.float32)
    idx = np.arange(RET_C, dtype=np.float32)
    rel = idx[:, None] - idx[None, :]
    causal = rel >= 0
    dec = np.where(causal[None], np.exp(lg[:, None, None] * np.where(causal, rel, 0.0)[None]), 0.0)
    xi = np.exp(lg[:, None] * (idx[None, :] + 1.0))
    zeta = np.exp(lg[:, None] * (RET_C - 1.0 - idx[None, :]))
    gamma_c = tuple(float(v) for v in np.exp(lg * RET_C).astype(np.float32))
    xi = np.broadcast_to(xi[:, :, None], (RET_HEADS, RET_C, RET_QK_DIM))
    zeta = np.broadcast_to(zeta[:, :, None], (RET_HEADS, RET_C, RET_QK_DIM))
    return (jnp.asarray(dec, F32), jnp.asarray(xi, F32), jnp.asarray(zeta, F32), gamma_c)


def _rotary_tables():
    freqs = ROPE_BASE ** (-jnp.arange(0, RET_QK_DIM, 2, dtype=F32) / RET_QK_DIM)
    ang = jnp.arange(SEQ, dtype=F32)[:, None] * freqs[None, :]
    cos, sin = jnp.cos(ang), jnp.sin(ang)
    cos2 = jnp.concatenate([cos, cos], axis=1)
    sin2 = jnp.concatenate([-sin, sin], axis=1)
    k_scale = RET_QK_DIM ** -0.5
    return cos2, sin2, cos2 * k_scale, sin2 * k_scale


def _mix_call(proj, x, mod_l, n2g, rot, ret, wro, pw, ps, wpo, wout, wrt, br, tri):
    dec, xi, zeta, gamma_c = ret
    ts = TILE
    n_s = SEQ // ts
    tok = lambda b, s: (b * n_s + s, 0)
    seq_tab = pl.BlockSpec((ts, RET_QK_DIM), lambda b, s: (s, 0))
    return pl.pallas_call(
        functools.partial(_mix_kernel, gamma_c),
        grid=(BATCH, n_s),
        in_specs=[
            pl.BlockSpec((ts, IN_W), tok),
            pl.BlockSpec((ts, D_MODEL), tok),
            pl.BlockSpec((1, 6, D_MODEL), lambda b, s: (b, 0, 0)),
            _const_spec((1, D_MODEL)),
            seq_tab, seq_tab, seq_tab, seq_tab,
            _const_spec((RET_HEADS, RET_C, RET_C)),
            _const_spec((RET_HEADS, RET_C, RET_QK_DIM)),
            _const_spec((RET_HEADS, RET_C, RET_QK_DIM)),
            _const_spec((RET_V_W, D_MODEL)),
            _const_spec((POOL_GROUPS, POOL_GROUP_DIM, POOL_GROUP_DIM)),
            _const_spec((1, POOL_W)),
            _const_spec((POOL_W, D_MODEL)),
            _const_spec((D_MODEL, D_MODEL)),
            _const_spec((N_EXPERTS, D_MODEL)),
            _const_spec((N_EXPERTS, 1)),
            _const_spec((ts, ts)),
        ],
        out_specs=[
            pl.BlockSpec((ts, D_MODEL), tok),
            pl.BlockSpec((1, SORT_R, D_MODEL), lambda b, s: (b * n_s + s, 0, 0)),
            pl.BlockSpec((4, ts), lambda b, s: (0, b * n_s + s)),
            pl.BlockSpec((1, N_EXPERTS, 128), lambda b, s: (b * n_s + s, 0, 0)),
        ],
        out_shape=[
            jax.ShapeDtypeStruct((N_TOK, D_MODEL), F32),
            jax.ShapeDtypeStruct((N_TILES, SORT_R, D_MODEL), BF16),
            jax.ShapeDtypeStruct((4, N_TOK), F32),
            jax.ShapeDtypeStruct((N_TILES, N_EXPERTS, 128), F32),
        ],
        scratch_shapes=[
            pltpu.VMEM((RET_HEADS, RET_QK_DIM, RET_V_DIM), F32),
            pltpu.VMEM((POOL_HALO, POOL_W), F32),
            pltpu.VMEM((ts, RET_V_W), BF16),
        ],
        compiler_params=pltpu.CompilerParams(
            dimension_semantics=("arbitrary", "arbitrary"), vmem_limit_bytes=VMEM_LIMIT),
        name="mixer_route",
    )(proj, x, mod_l, n2g, *rot, dec, xi, zeta, wro, pw, ps, wpo, wout, wrt, br, tri)


def _plan(n_pad):
    n_pad = n_pad.astype(I32)
    units = (n_pad // ALIGN).T
    run_row = (jnp.cumsum(n_pad, axis=1) - n_pad).T
    unit_start = jnp.cumsum(units, axis=1) - units
    total = units.sum(axis=1)
    n_blk = (total + UNITS_PER_BLK - 1) // UNITS_PER_BLK
    blk_end = jnp.cumsum(n_blk)
    blk_start = blk_end - n_blk
    n_used = blk_end[-1:]
    b = jnp.arange(N_BLOCKS + 1, dtype=I32)
    e_b = jnp.minimum((b[:, None] >= blk_end[None, :]).sum(axis=1), N_EXPERTS - 1).astype(I32)
    oh_e = (e_b[:, None] == jnp.arange(N_EXPERTS, dtype=I32)[None, :]).astype(I32)
    pick = lambda tab: (oh_e[:, :, None] * tab[None, :, :]).sum(axis=1)
    unit_start_b, units_b, run_row_b = pick(unit_start), pick(units), pick(run_row)
    total_b = (oh_e * total[None, :]).sum(axis=1)
    first_q = (b - (oh_e * blk_start[None, :]).sum(axis=1)) * UNITS_PER_BLK
    n_valid = jnp.where(b < n_used[0], jnp.clip(total_b - first_q, 0, UNITS_PER_BLK), 0).astype(I32)
    q = first_q[:, None] + jnp.arange(UNITS_PER_BLK, dtype=I32)[None, :]
    tile = jnp.minimum(((unit_start_b + units_b)[:, None, :] <= q[:, :, None]).sum(axis=2), N_TILES - 1)
    oh_t = (tile[:, :, None] == jnp.arange(N_TILES, dtype=I32)[None, None, :]).astype(I32)
    base = jnp.arange(N_TILES, dtype=I32)[None, :] * SORT_R + run_row_b - unit_start_b * ALIGN
    row = q * ALIGN + (oh_t * base[:, None, :]).sum(axis=2)
    m = jnp.arange(UNITS_PER_BLK, dtype=I32)[None, :]
    valid = m < n_valid[:, None]
    pad_rank = jnp.clip(m - n_valid[:, None], 0, UNITS_PER_BLK - 2)
    dump = N_TILES * SORT_R + (e_b[:, None] * (UNITS_PER_BLK - 1) + pad_rank) * ALIGN
    src = jnp.where(valid, row, 0).astype(I32).reshape(-1)
    dst = jnp.where(valid, row, dump).astype(I32).reshape(-1)
    return e_b, n_used.astype(I32), src, dst


def _expert_kernel(be_ref, nu_ref, src_ref, dst_ref,
                   xs_hbm, wg_ref, wu_ref, wd_ref, ys_in_hbm, ys_hbm,
                   xbuf, ybuf, wg_bf, wu_bf, wd_bf, sem_in, sem_out):
    del ys_in_hbm
    b = pl.program_id(0)
    n_used = nu_ref[0]
    slot = b % 2

    def start_in(blk, sl):
        rows = [pl.multiple_of(src_ref[blk * UNITS_PER_BLK + m], ALIGN) for m in range(UNITS_PER_BLK)]
        for m in range(UNITS_PER_BLK):
            pltpu.make_async_copy(xs_hbm.at[pl.ds(rows[m], ALIGN), :],
                                  xbuf.at[sl, pl.ds(m * ALIGN, ALIGN), :], sem_in.at[sl]).start()

    def start_out(blk, sl):
        rows = [pl.multiple_of(dst_ref[blk * UNITS_PER_BLK + m], ALIGN) for m in range(UNITS_PER_BLK)]
        for m in range(UNITS_PER_BLK):
            pltpu.make_async_copy(ybuf.at[sl, pl.ds(m * ALIGN, ALIGN), :],
                                  ys_hbm.at[pl.ds(rows[m], ALIGN), :], sem_out.at[sl]).start()

    def wait_in(sl):
        pltpu.make_async_copy(xs_hbm.at[pl.ds(0, EXP_BLK), :], xbuf.at[sl], sem_in.at[sl]).wait()

    def wait_out(sl):
        pltpu.make_async_copy(ybuf.at[sl], ys_hbm.at[pl.ds(0, EXP_BLK), :], sem_out.at[sl]).wait()

    @pl.when(b == 0)
    def _():
        start_in(0, 0)

    @pl.when(b < n_used)
    def _():
        wait_in(slot)

        @pl.when(b >= 2)
        def _():
            wait_out(slot)

        @pl.when((b == 0) | (be_ref[b] != be_ref[jnp.maximum(b - 1, 0)]))
        def _():
            wg_bf[...] = wg_ref[0].astype(BF16)
            wu_bf[...] = wu_ref[0].astype(BF16)
            wd_bf[...] = wd_ref[0].astype(BF16)

        start_in(b + 1, 1 - slot)
        xb = xbuf[slot]
        a = _dot(xb, wg_bf[...])
        u = _dot(xb, wu_bf[...])
        ybuf[slot] = _dot((jax.nn.silu(a) * u).astype(BF16), wd_bf[...]).astype(BF16)
        start_out(b, slot)

        @pl.when(b == n_used - 1)
        def _():
            wait_in(1 - slot)

            @pl.when(b >= 1)
            def _():
                wait_out(1 - slot)
            wait_out(slot)


def _expert_call(block_e, n_used, src, dst, xs, wg, wu, wd, ys):
    def w_map(i, be, nu, sr, ds):
        return (be[i], 0, 0)

    grid_spec = pltpu.PrefetchScalarGridSpec(
        num_scalar_prefetch=4,
        grid=(N_BLOCKS,),
        in_specs=[
            pl.BlockSpec(memory_space=pl.ANY),
            pl.BlockSpec((1, D_MODEL, D_EXPERT), w_map),
            pl.BlockSpec((1, D_MODEL, D_EXPERT), w_map),
            pl.BlockSpec((1, D_EXPERT, D_MODEL), w_map),
            pl.BlockSpec(memory_space=pl.ANY),
        ],
        out_specs=pl.BlockSpec(memory_space=pl.ANY),
        scratch_shapes=[
            pltpu.VMEM((2, EXP_BLK, D_MODEL), BF16),
            pltpu.VMEM((2, EXP_BLK, D_MODEL), BF16),
            pltpu.VMEM((D_MODEL, D_EXPERT), BF16),
            pltpu.VMEM((D_MODEL, D_EXPERT), BF16),
            pltpu.VMEM((D_EXPERT, D_MODEL), BF16),
            pltpu.SemaphoreType.DMA((2,)),
            pltpu.SemaphoreType.DMA((2,)),
        ],
    )
    return pl.pallas_call(
        _expert_kernel,
        grid_spec=grid_spec,
        out_shape=jax.ShapeDtypeStruct((YS_ROWS, D_MODEL), BF16),
        input_output_aliases={8: 0},
        compiler_params=pltpu.CompilerParams(
            dimension_semantics=("arbitrary",), vmem_limit_bytes=VMEM_LIMIT, has_side_effects=True),
        name="moe_experts",
    )(block_e, n_used, src, dst, xs, wg, wu, wd, ys)


def kernel(x, c, w_ada, b_ada, norm1_g, norm2_g, w_in, w_ret_o, pool_w, pool_scale, w_pool_o, w_out,
           w_router, b_router, w_e_gate, w_e_up, w_e_down, final_g):
    assert x.shape == (BATCH, SEQ, D_MODEL) and w_in.shape == (DEPTH, D_MODEL, IN_W)
    xt = x.reshape(N_TOK, D_MODEL)
    mod = _ada_call(c, w_ada, b_ada).reshape(DEPTH, BATCH, 6, D_MODEL)

    rot = _rotary_tables()
    ret = _retention_tables()
    tri = (jnp.arange(TILE)[:, None] < jnp.arange(TILE)[None, :]).astype(BF16)
    wrt = w_router.T.astype(BF16)
    br = b_router.astype(F32).reshape(N_EXPERTS, 1)
    ys = jnp.zeros((YS_ROWS, D_MODEL), BF16)

    moe = None
    for l in range(DEPTH):
        g1 = norm1_g[l].reshape(1, D_MODEL)
        if moe is None:
            proj = _proj_call(xt, mod[l], g1, w_in[l].astype(BF16))
        else:
            xt, proj = _proj_call(xt, mod[l], g1, w_in[l].astype(BF16), moe)
        xt, xs, lw, n_pad = _mix_call(
            proj, xt, mod[l], norm2_g[l].reshape(1, D_MODEL), rot, ret,
            w_ret_o[l].astype(BF16), pool_w[l].astype(BF16), pool_scale[l].reshape(1, POOL_W),
            w_pool_o[l].astype(BF16), w_out[l].astype(BF16), wrt, br, tri)
        block_e, n_used, src, dst = _plan(n_pad[:, :, 0])
        ys = _expert_call(block_e, n_used, src, dst, xs.reshape(N_TILES * SORT_R, D_MODEL),
                          w_e_gate[l], w_e_up[l], w_e_down[l], ys)
        moe = (lw.T, ys, mod[l])
    out = _final_call(xt, *moe, final_g.reshape(1, D_MODEL))
    return out.reshape(BATCH, SEQ, D_MODEL)
```

```python
import functools

import numpy as np
import jax
import jax.numpy as jnp
from jax import lax
from jax.experimental import pallas as pl
from jax.experimental.pallas import tpu as pltpu

D_MODEL = 1024
BATCH = 16
SEQ = 2048
DEPTH = 4
N_TOK = BATCH * SEQ

RET_HEADS = 4
RET_QK_DIM = 128
RET_V_DIM = 256
RET_QK_W = RET_HEADS * RET_QK_DIM
RET_V_W = RET_HEADS * RET_V_DIM
ROPE_BASE = 10000.0
POOL_WINDOWS = (2, 4, 8, 16)
POOL_GROUPS = 4
POOL_GROUP_DIM = 128
POOL_W = POOL_GROUPS * POOL_GROUP_DIM
POOL_HALO = 16
IN_W = 2 * RET_QK_W + 2 * RET_V_W + POOL_W + 2 * D_MODEL
OFF_Q = 0
OFF_K = OFF_Q + RET_QK_W
OFF_V = OFF_K + RET_QK_W
OFF_G = OFF_V + RET_V_W
OFF_U = OFF_G + RET_V_W
OFF_AR = OFF_U + POOL_W
OFF_AP = OFF_AR + D_MODEL

N_EXPERTS = 16
N_GROUPS = 4
EXPERTS_PER_GROUP = 4
TOP_K = 2
D_EXPERT = D_MODEL // 2
EPS = 1e-6

TILE = 512
PROJ_TN = 512
RET_C = 256
N_TILES = N_TOK // TILE
ALIGN = 16
SORT_R = -(-(TOP_K * TILE + N_EXPERTS * (ALIGN - 1)) // 128) * 128
SORT_TN = 256
EXP_BLK = 512
UNITS_PER_BLK = EXP_BLK // ALIGN
MAX_UNITS = N_TILES * ((TOP_K * TILE + N_EXPERTS * (ALIGN - 1)) // ALIGN)
N_BLOCKS = -(-MAX_UNITS // UNITS_PER_BLK) + N_EXPERTS
DUMP_UNITS = N_EXPERTS * (UNITS_PER_BLK - 1)
YS_ROWS = (N_TILES + -(-DUMP_UNITS * ALIGN // SORT_R)) * SORT_R
VMEM_LIMIT = 56 * 1024 * 1024

F32 = jnp.float32
BF16 = jnp.bfloat16
I32 = jnp.int32


def _const_spec(shape):
    zeros = (0,) * len(shape)
    return pl.BlockSpec(shape, lambda *_: zeros, pipeline_mode=pl.Buffered(1))


def _rms(xf):
    return xf * lax.rsqrt(jnp.mean(xf * xf, axis=-1, keepdims=True) + EPS)


def _dot(a, b):
    return jnp.dot(a, b, preferred_element_type=F32)


def _ada_kernel(c_ref, w_ref, b_ref, o_ref):
    cond = jax.nn.silu(c_ref[...]).astype(BF16)
    o_ref[0] = _dot(cond, w_ref[0].astype(BF16)) + b_ref[0]


def _ada_call(c, w_ada, b_ada):
    n_col = 6
    return pl.pallas_call(
        _ada_kernel,
        grid=(DEPTH, n_col),
        in_specs=[
            pl.BlockSpec((BATCH, D_MODEL), lambda l, j: (0, 0)),
            pl.BlockSpec((1, D_MODEL, D_MODEL), lambda l, j: (l, 0, j)),
            pl.BlockSpec((1, 1, D_MODEL), lambda l, j: (l, 0, j)),
        ],
        out_specs=pl.BlockSpec((1, BATCH, D_MODEL), lambda l, j: (l, 0, j)),
        out_shape=jax.ShapeDtypeStruct((DEPTH, BATCH, n_col * D_MODEL), F32),
        compiler_params=pltpu.CompilerParams(
            dimension_semantics=("arbitrary", "arbitrary"), vmem_limit_bytes=VMEM_LIMIT),
        name="ada_mod",
    )(c, w_ada, b_ada.reshape(DEPTH, 1, n_col * D_MODEL))


def _moe_combine(x, cols, ys, gate):
    lane = lax.broadcasted_iota(I32, (TILE, SORT_R), 1)
    l0 = cols[:, 0:1].astype(I32)
    l1 = cols[:, 1:2].astype(I32)
    pw = jnp.where(lane == l0, cols[:, 2:3], 0.0) + jnp.where(lane == l1, cols[:, 3:4], 0.0)
    return x + gate * _dot(pw.astype(BF16), ys)


def _proj_kernel(combine, *refs):
    if combine:
        x_ref, cols_ref, ys_ref, modp_ref, mod_ref, g_ref, w_ref, xo_ref, o_ref = refs
        x = _moe_combine(x_ref[...], cols_ref[...], ys_ref[...], modp_ref[0, 5:6, :])
        xo_ref[...] = x
    else:
        x_ref, mod_ref, g_ref, w_ref, o_ref = refs
        x = x_ref[...]
    h = (_rms(x) * g_ref[...]) * (1.0 + mod_ref[0, 1:2, :]) + mod_ref[0, 0:1, :]
    hb = h.astype(BF16)
    for n in range(IN_W // PROJ_TN):
        cols = slice(n * PROJ_TN, (n + 1) * PROJ_TN)
        o_ref[:, cols] = _dot(hb, w_ref[:, cols]).astype(BF16)


def _proj_call(x, mod_l, g, w_in_l, moe=None):
    tiles_per_b = SEQ // TILE
    tok = pl.BlockSpec((TILE, D_MODEL), lambda i: (i, 0))
    mod_spec = pl.BlockSpec((1, 6, D_MODEL), lambda i: (i // tiles_per_b, 0, 0))
    proj_spec = pl.BlockSpec((TILE, IN_W), lambda i: (i, 0))
    proj_shape = jax.ShapeDtypeStruct((N_TOK, IN_W), BF16)
    tail_specs = [mod_spec, _const_spec((1, D_MODEL)), _const_spec((D_MODEL, IN_W))]
    if moe is None:
        in_specs = [tok] + tail_specs
        args = (x, mod_l, g, w_in_l)
        out_specs, out_shape = proj_spec, proj_shape
    else:
        cols, ys, mod_prev = moe
        in_specs = [tok, pl.BlockSpec((TILE, 4), lambda i: (i, 0)),
                    pl.BlockSpec((SORT_R, D_MODEL), lambda i: (i, 0)), mod_spec] + tail_specs
        args = (x, cols, ys, mod_prev, mod_l, g, w_in_l)
        out_specs = [tok, proj_spec]
        out_shape = [jax.ShapeDtypeStruct((N_TOK, D_MODEL), F32), proj_shape]
    return pl.pallas_call(
        functools.partial(_proj_kernel, moe is not None),
        grid=(N_TILES,),
        in_specs=in_specs,
        out_specs=out_specs,
        out_shape=out_shape,
        compiler_params=pltpu.CompilerParams(
            dimension_semantics=("arbitrary",), vmem_limit_bytes=VMEM_LIMIT),
        name="in_proj",
    )(*args)


def _final_kernel(x_ref, cols_ref, ys_ref, modp_ref, g_ref, o_ref):
    x = _moe_combine(x_ref[...], cols_ref[...], ys_ref[...], modp_ref[0, 5:6, :])
    o_ref[...] = _rms(x) * g_ref[...]


def _final_call(x, cols, ys, mod_prev, final_g):
    tiles_per_b = SEQ // TILE
    tok = pl.BlockSpec((TILE, D_MODEL), lambda i: (i, 0))
    return pl.pallas_call(
        _final_kernel,
        grid=(N_TILES,),
        in_specs=[tok, pl.BlockSpec((TILE, 4), lambda i: (i, 0)),
                  pl.BlockSpec((SORT_R, D_MODEL), lambda i: (i, 0)),
                  pl.BlockSpec((1, 6, D_MODEL), lambda i: (i // tiles_per_b, 0, 0)),
                  _const_spec((1, D_MODEL))],
        out_specs=tok,
        out_shape=jax.ShapeDtypeStruct((N_TOK, D_MODEL), F32),
        compiler_params=pltpu.CompilerParams(
            dimension_semantics=("arbitrary",), vmem_limit_bytes=VMEM_LIMIT),
        name="final_norm",
    )(x, cols, ys, mod_prev, final_g)


def _top2_sum(a, b, c, d):
    hi1, lo1 = jnp.maximum(a, b), jnp.minimum(a, b)
    hi2, lo2 = jnp.maximum(c, d), jnp.minimum(c, d)
    return jnp.maximum(hi1, hi2) + jnp.maximum(jnp.minimum(hi1, hi2), jnp.maximum(lo1, lo2))


def _first_argmax(vals):
    best = vals[0]
    idx = jnp.zeros(best.shape, I32)
    for j in range(1, len(vals)):
        better = vals[j] > best
        idx = jnp.where(better, j, idx)
        best = jnp.where(better, vals[j], best)
    return idx, best


def _mix_kernel(gamma_c,
                proj_ref, x_ref, mod_ref, n2g_ref, cq_ref, sq_ref, ck_ref, sk_ref,
                dec_ref, xi_ref, zeta_ref, wro_ref, pw_ref, ps_ref, wpo_ref, wout_ref,
                wrt_ref, br_ref, tri_ref,
                xo_ref, xs_ref, lw_ref, np_ref,
                state_ref, halo_ref, gated_ref, wro_bf, pw_bf, wpo_bf, wout_bf):
    s = pl.program_id(1)
    ts = TILE

    @pl.when((pl.program_id(0) == 0) & (s == 0))
    def _():
        wro_bf[...] = wro_ref[0].astype(BF16)
        pw_bf[...] = pw_ref[0].astype(BF16)
        wpo_bf[...] = wpo_ref[0].astype(BF16)
        wout_bf[...] = wout_ref[0].astype(BF16)

    @pl.when(s == 0)
    def _():
        state_ref[...] = jnp.zeros_like(state_ref)
        halo_ref[...] = jnp.zeros_like(halo_ref)

    for j in range(ts // RET_C):
        rows = slice(j * RET_C, (j + 1) * RET_C)
        cq, sq = cq_ref[rows, :], sq_ref[rows, :]
        ck, sk = ck_ref[rows, :], sk_ref[rows, :]
        for h in range(RET_HEADS):
            qh = proj_ref[rows, OFF_Q + h * RET_QK_DIM:OFF_Q + (h + 1) * RET_QK_DIM].astype(F32)
            kh = proj_ref[rows, OFF_K + h * RET_QK_DIM:OFF_K + (h + 1) * RET_QK_DIM].astype(F32)
            vh = proj_ref[rows, OFF_V + h * RET_V_DIM:OFF_V + (h + 1) * RET_V_DIM]
            qh = qh * cq + pltpu.roll(qh, RET_QK_DIM // 2, axis=1) * sq
            kh = kh * ck + pltpu.roll(kh, RET_QK_DIM // 2, axis=1) * sk
            sc = lax.dot_general(qh.astype(BF16), kh.astype(BF16), (((1,), (1,)), ((), ())),
                                 preferred_element_type=F32) * dec_ref[h]
            st = state_ref[h]
            y = _dot(sc.astype(BF16), vh) + _dot((qh * xi_ref[h]).astype(BF16), st.astype(BF16))
            kv = lax.dot_general((kh * zeta_ref[h]).astype(BF16), vh, (((0,), (0,)), ((), ())),
                                 preferred_element_type=F32)
            state_ref[h] = st * gamma_c[h] + kv
            gh = proj_ref[rows, OFF_G + h * RET_V_DIM:OFF_G + (h + 1) * RET_V_DIM].astype(F32)
            gated_ref[rows, h * RET_V_DIM:(h + 1) * RET_V_DIM] = (jax.nn.silu(gh) * _rms(y)).astype(BF16)
    y_ret = _dot(gated_ref[...], wro_bf[...])

    u = proj_ref[:, OFF_U:OFF_U + POOL_W].astype(F32)
    ext = jnp.concatenate([halo_ref[...], u], axis=0)
    halo_ref[...] = u[ts - POOL_HALO:, :]
    pos = s * ts + lax.broadcasted_iota(I32, (ts, 1), 0)
    pooled = []
    for g, win in enumerate(POOL_WINDOWS):
        cols = slice(g * POOL_GROUP_DIM, (g + 1) * POOL_GROUP_DIM)
        acc = ext[:, cols]
        shift = 1
        while shift < win:
            acc = acc + pltpu.roll(acc, shift, axis=0)
            shift *= 2
        count = jnp.minimum(pos + 1, win).astype(F32)
        mix = acc[POOL_HALO:, :] / count - u[:, cols]
        pooled.append(_dot(mix.astype(BF16), pw_bf[g]))
    y_pool = _dot((jnp.concatenate(pooled, axis=1) * ps_ref[...]).astype(BF16), wpo_bf[...])

    a_ret = proj_ref[:, OFF_AR:OFF_AR + D_MODEL].astype(F32)
    a_pool = proj_ref[:, OFF_AP:OFF_AP + D_MODEL].astype(F32)
    merged = jax.nn.sigmoid(a_ret) * y_ret + jax.nn.sigmoid(a_pool) * y_pool
    x_new = x_ref[...] + mod_ref[0, 2:3, :] * _dot(merged.astype(BF16), wout_bf[...])
    xo_ref[...] = x_new

    h2 = (_rms(x_new) * n2g_ref[...]) * (1.0 + mod_ref[0, 4:5, :]) + mod_ref[0, 3:4, :]
    h2b = h2.astype(BF16)

    logits = lax.dot_general(wrt_ref[...], h2b, (((1,), (1,)), ((), ())),
                             preferred_element_type=F32) + br_ref[...]
    p = jnp.exp(logits - jnp.max(logits, axis=0, keepdims=True))
    probs = p / jnp.sum(p, axis=0, keepdims=True)
    pr = [probs[e:e + 1, :] for e in range(N_EXPERTS)]
    scores = [_top2_sum(*pr[g * EXPERTS_PER_GROUP:(g + 1) * EXPERTS_PER_GROUP]) for g in range(N_GROUPS)]
    grp, _ = _first_argmax(scores)
    in_grp = []
    for jj in range(EXPERTS_PER_GROUP):
        v = pr[jj]
        for g in range(1, N_GROUPS):
            v = jnp.where(grp == g, pr[g * EXPERTS_PER_GROUP + jj], v)
        in_grp.append(v)
    i0, p0 = _first_argmax(in_grp)
    i1, p1 = _first_argmax([jnp.where(i0 == jj, -1.0, in_grp[jj]) for jj in range(EXPERTS_PER_GROUP)])
    e0 = grp * EXPERTS_PER_GROUP + i0
    e1 = grp * EXPERTS_PER_GROUP + i1
    den = p0 + p1

    erow = lax.broadcasted_iota(I32, (N_EXPERTS, ts), 0)
    oh0 = erow == e0
    oh1 = erow == e1
    ohf = jnp.where(oh0 | oh1, 1.0, 0.0)
    n_e = jnp.sum(ohf, axis=1, keepdims=True)
    n_pad = jnp.broadcast_to(jnp.floor((n_e + (ALIGN - 1.0)) * (1.0 / ALIGN)) * ALIGN, (N_EXPERTS, 128))
    np_ref[0] = n_pad
    erow128 = lax.broadcasted_iota(I32, (N_EXPERTS, 128), 0)
    run_end = n_pad
    shift = 1
    while shift < N_EXPERTS:
        run_end = run_end + jnp.where(erow128 >= shift, pltpu.roll(run_end, shift, axis=0), 0.0)
        shift *= 2
    run_start = (run_end - n_pad)[:, 0:1]
    row_of = _dot(ohf.astype(BF16), tri_ref[...]) + run_start
    l0 = jnp.sum(jnp.where(oh0, row_of, 0.0), axis=0, keepdims=True)
    l1 = jnp.sum(jnp.where(oh1, row_of, 0.0), axis=0, keepdims=True)
    lw_ref[...] = jnp.concatenate([l0, l1, p0 / den, p1 / den], axis=0)

    srow = lax.broadcasted_iota(I32, (SORT_R, ts), 0)
    perm = jnp.where((srow == l0.astype(I32)) | (srow == l1.astype(I32)), 1.0, 0.0).astype(BF16)
    for n in range(D_MODEL // SORT_TN):
        cols = slice(n * SORT_TN, (n + 1) * SORT_TN)
        xs_ref[0, :, cols] = _dot(perm, h2b[:, cols]).astype(BF16)


def _retention_tables():
    lg = np.log(1.0 - 2.0 ** (-5.0 - np.arange(RET_HEADS, dtype=np.float32))).astype(np.float32)
    idx = np.arange(RET_C, dtype=np.float32)
    rel = idx[:, None] - idx[None, :]
    causal = rel >= 0
    dec = np.where(causal[None], np.exp(lg[:, None, None] * np.where(causal, rel, 0.0)[None]), 0.0)
    xi = np.exp(lg[:, None] * (idx[None, :] + 1.0))
    zeta = np.exp(lg[:, None] * (RET_C - 1.0 - idx[None, :]))
    gamma_c = tuple(float(v) for v in np.exp(lg * RET_C).astype(np.float32))
    xi = np.broadcast_to(xi[:, :, None], (RET_HEADS, RET_C, RET_QK_DIM))
    zeta = np.broadcast_to(zeta[:, :, None], (RET_HEADS, RET_C, RET_QK_DIM))
    return (jnp.asarray(dec, F32), jnp.asarray(xi, F32), jnp.asarray(zeta, F32), gamma_c)


def _rotary_tables():
    freqs = ROPE_BASE ** (-jnp.arange(0, RET_QK_DIM, 2, dtype=F32) / RET_QK_DIM)
    ang = jnp.arange(SEQ, dtype=F32)[:, None] * freqs[None, :]
    cos, sin = jnp.cos(ang), jnp.sin(ang)
    cos2 = jnp.concatenate([cos, cos], axis=1)
    sin2 = jnp.concatenate([-sin, sin], axis=1)
    k_scale = RET_QK_DIM ** -0.5
    return cos2, sin2, cos2 * k_scale, sin2 * k_scale


def _mix_call(layer, proj, x, mod_l, n2g, rot, ret, wro, pw, ps, wpo, wout, wrt, br, tri):
    dec, xi, zeta, gamma_c = ret

    def layer_spec(shape):
        zeros = (0,) * len(shape)
        return pl.BlockSpec((1,) + shape, lambda b, s: (layer,) + zeros, pipeline_mode=pl.Buffered(1))

    ts = TILE
    n_s = SEQ // ts
    tok = lambda b, s: (b * n_s + s, 0)
    seq_tab = pl.BlockSpec((ts, RET_QK_DIM), lambda b, s: (s, 0))
    return pl.pallas_call(
        functools.partial(_mix_kernel, gamma_c),
        grid=(BATCH, n_s),
        in_specs=[
            pl.BlockSpec((ts, IN_W), tok),
            pl.BlockSpec((ts, D_MODEL), tok),
            pl.BlockSpec((1, 6, D_MODEL), lambda b, s: (b, 0, 0)),
            _const_spec((1, D_MODEL)),
            seq_tab, seq_tab, seq_tab, seq_tab,
            _const_spec((RET_HEADS, RET_C, RET_C)),
            _const_spec((RET_HEADS, RET_C, RET_QK_DIM)),
            _const_spec((RET_HEADS, RET_C, RET_QK_DIM)),
            layer_spec((RET_V_W, D_MODEL)),
            layer_spec((POOL_GROUPS, POOL_GROUP_DIM, POOL_GROUP_DIM)),
            _const_spec((1, POOL_W)),
            layer_spec((POOL_W, D_MODEL)),
            layer_spec((D_MODEL, D_MODEL)),
            _const_spec((N_EXPERTS, D_MODEL)),
            _const_spec((N_EXPERTS, 1)),
            _const_spec((ts, ts)),
        ],
        out_specs=[
            pl.BlockSpec((ts, D_MODEL), tok),
            pl.BlockSpec((1, SORT_R, D_MODEL), lambda b, s: (b * n_s + s, 0, 0)),
            pl.BlockSpec((4, ts), lambda b, s: (0, b * n_s + s)),
            pl.BlockSpec((1, N_EXPERTS, 128), lambda b, s: (b * n_s + s, 0, 0)),
        ],
        out_shape=[
            jax.ShapeDtypeStruct((N_TOK, D_MODEL), F32),
            jax.ShapeDtypeStruct((N_TILES, SORT_R, D_MODEL), BF16),
            jax.ShapeDtypeStruct((4, N_TOK), F32),
            jax.ShapeDtypeStruct((N_TILES, N_EXPERTS, 128), F32),
        ],
        scratch_shapes=[
            pltpu.VMEM((RET_HEADS, RET_QK_DIM, RET_V_DIM), F32),
            pltpu.VMEM((POOL_HALO, POOL_W), F32),
            pltpu.VMEM((ts, RET_V_W), BF16),
            pltpu.VMEM((RET_V_W, D_MODEL), BF16),
            pltpu.VMEM((POOL_GROUPS, POOL_GROUP_DIM, POOL_GROUP_DIM), BF16),
            pltpu.VMEM((POOL_W, D_MODEL), BF16),
            pltpu.VMEM((D_MODEL, D_MODEL), BF16),
        ],
        compiler_params=pltpu.CompilerParams(
            dimension_semantics=("arbitrary", "arbitrary"), vmem_limit_bytes=VMEM_LIMIT),
        name="mixer_route",
    )(proj, x, mod_l, n2g, *rot, dec, xi, zeta, wro, pw, ps, wpo, wout, wrt, br, tri)


def _plan(n_pad):
    n_pad = n_pad.astype(I32)
    units = (n_pad // ALIGN).T
    run_row = (jnp.cumsum(n_pad, axis=1) - n_pad).T
    unit_start = jnp.cumsum(units, axis=1) - units
    total = units.sum(axis=1)
    n_blk = (total + UNITS_PER_BLK - 1) // UNITS_PER_BLK
    blk_end = jnp.cumsum(n_blk)
    blk_start = blk_end - n_blk
    n_used = blk_end[-1:]
    b = jnp.arange(N_BLOCKS + 1, dtype=I32)
    e_b = jnp.minimum((b[:, None] >= blk_end[None, :]).sum(axis=1), N_EXPERTS - 1).astype(I32)
    oh_e = (e_b[:, None] == jnp.arange(N_EXPERTS, dtype=I32)[None, :]).astype(I32)
    pick = lambda tab: (oh_e[:, :, None] * tab[None, :, :]).sum(axis=1)
    unit_start_b, units_b, run_row_b = pick(unit_start), pick(units), pick(run_row)
    total_b = (oh_e * total[None, :]).sum(axis=1)
    first_q = (b - (oh_e * blk_start[None, :]).sum(axis=1)) * UNITS_PER_BLK
    n_valid = jnp.where(b < n_used[0], jnp.clip(total_b - first_q, 0, UNITS_PER_BLK), 0).astype(I32)
    q = first_q[:, None] + jnp.arange(UNITS_PER_BLK, dtype=I32)[None, :]
    tile = jnp.minimum(((unit_start_b + units_b)[:, None, :] <= q[:, :, None]).sum(axis=2), N_TILES - 1)
    oh_t = (tile[:, :, None] == jnp.arange(N_TILES, dtype=I32)[None, None, :]).astype(I32)
    base = jnp.arange(N_TILES, dtype=I32)[None, :] * SORT_R + run_row_b - unit_start_b * ALIGN
    row = q * ALIGN + (oh_t * base[:, None, :]).sum(axis=2)
    m = jnp.arange(UNITS_PER_BLK, dtype=I32)[None, :]
    valid = m < n_valid[:, None]
    pad_rank = jnp.clip(m - n_valid[:, None], 0, UNITS_PER_BLK - 2)
    dump = N_TILES * SORT_R + (e_b[:, None] * (UNITS_PER_BLK - 1) + pad_rank) * ALIGN
    src = jnp.where(valid, row, 0).astype(I32).reshape(-1)
    dst = jnp.where(valid, row, dump).astype(I32).reshape(-1)
    return e_b, n_used.astype(I32), src, dst


def _expert_kernel(be_ref, nu_ref, src_ref, dst_ref,
                   xs_hbm, wg_ref, wu_ref, wd_ref, ys_in_hbm, ys_hbm,
                   xbuf, ybuf, wg_bf, wu_bf, wd_bf, sem_in, sem_out):
    del ys_in_hbm
    b = pl.program_id(0)
    n_used = nu_ref[0]
    slot = b % 2

    def start_in(blk, sl):
        rows = [pl.multiple_of(src_ref[blk * UNITS_PER_BLK + m], ALIGN) for m in range(UNITS_PER_BLK)]
        for m in range(UNITS_PER_BLK):
            pltpu.make_async_copy(xs_hbm.at[pl.ds(rows[m], ALIGN), :],
                                  xbuf.at[sl, pl.ds(m * ALIGN, ALIGN), :], sem_in.at[sl]).start()

    def start_out(blk, sl):
        rows = [pl.multiple_of(dst_ref[blk * UNITS_PER_BLK + m], ALIGN) for m in range(UNITS_PER_BLK)]
        for m in range(UNITS_PER_BLK):
            pltpu.make_async_copy(ybuf.at[sl, pl.ds(m * ALIGN, ALIGN), :],
                                  ys_hbm.at[pl.ds(rows[m], ALIGN), :], sem_out.at[sl]).start()

    def wait_in(sl):
        pltpu.make_async_copy(xs_hbm.at[pl.ds(0, EXP_BLK), :], xbuf.at[sl], sem_in.at[sl]).wait()

    def wait_out(sl):
        pltpu.make_async_copy(ybuf.at[sl], ys_hbm.at[pl.ds(0, EXP_BLK), :], sem_out.at[sl]).wait()

    @pl.when(b == 0)
    def _():
        start_in(0, 0)

    @pl.when(b < n_used)
    def _():
        wait_in(slot)

        @pl.when(b >= 2)
        def _():
            wait_out(slot)

        @pl.when((b == 0) | (be_ref[b] != be_ref[jnp.maximum(b - 1, 0)]))
        def _():
            wg_bf[...] = wg_ref[0, 0].astype(BF16)
            wu_bf[...] = wu_ref[0, 0].astype(BF16)
            wd_bf[...] = wd_ref[0, 0].astype(BF16)

        start_in(b + 1, 1 - slot)
        xb = xbuf[slot]
        a = _dot(xb, wg_bf[...])
        u = _dot(xb, wu_bf[...])
        ybuf[slot] = _dot((jax.nn.silu(a) * u).astype(BF16), wd_bf[...]).astype(BF16)
        start_out(b, slot)

        @pl.when(b == n_used - 1)
        def _():
            wait_in(1 - slot)

            @pl.when(b >= 1)
            def _():
                wait_out(1 - slot)
            wait_out(slot)


def _expert_call(layer, block_e, n_used, src, dst, xs, wg, wu, wd, ys):
    def w_map(i, be, nu, sr, ds):
        return (layer, be[i], 0, 0)

    grid_spec = pltpu.PrefetchScalarGridSpec(
        num_scalar_prefetch=4,
        grid=(N_BLOCKS,),
        in_specs=[
            pl.BlockSpec(memory_space=pl.ANY),
            pl.BlockSpec((1, 1, D_MODEL, D_EXPERT), w_map),
            pl.BlockSpec((1, 1, D_MODEL, D_EXPERT), w_map),
            pl.BlockSpec((1, 1, D_EXPERT, D_MODEL), w_map),
            pl.BlockSpec(memory_space=pl.ANY),
        ],
        out_specs=pl.BlockSpec(memory_space=pl.ANY),
        scratch_shapes=[
            pltpu.VMEM((2, EXP_BLK, D_MODEL), BF16),
            pltpu.VMEM((2, EXP_BLK, D_MODEL), BF16),
            pltpu.VMEM((D_MODEL, D_EXPERT), BF16),
            pltpu.VMEM((D_MODEL, D_EXPERT), BF16),
            pltpu.VMEM((D_EXPERT, D_MODEL), BF16),
            pltpu.SemaphoreType.DMA((2,)),
            pltpu.SemaphoreType.DMA((2,)),
        ],
    )
    return pl.pallas_call(
        _expert_kernel,
        grid_spec=grid_spec,
        out_shape=jax.ShapeDtypeStruct((YS_ROWS, D_MODEL), BF16),
        input_output_aliases={8: 0},
        compiler_params=pltpu.CompilerParams(
            dimension_semantics=("arbitrary",), vmem_limit_bytes=VMEM_LIMIT, has_side_effects=True),
        name="moe_experts",
    )(block_e, n_used, src, dst, xs, wg, wu, wd, ys)


def kernel(x, c, w_ada, b_ada, norm1_g, norm2_g, w_in, w_ret_o, pool_w, pool_scale, w_pool_o, w_out,
           w_router, b_router, w_e_gate, w_e_up, w_e_down, final_g):
    assert x.shape == (BATCH, SEQ, D_MODEL) and w_in.shape == (DEPTH, D_MODEL, IN_W)
    xt = x.reshape(N_TOK, D_MODEL)
    mod = _ada_call(c, w_ada, b_ada).reshape(DEPTH, BATCH, 6, D_MODEL)

    rot = _rotary_tables()
    ret = _retention_tables()
    tri = (jnp.arange(TILE)[:, None] < jnp.arange(TILE)[None, :]).astype(BF16)
    wrt = w_router.T.astype(BF16)
    br = b_router.astype(F32).reshape(N_EXPERTS, 1)
    ys = jnp.zeros((YS_ROWS, D_MODEL), BF16)

    moe = None
    for l in range(DEPTH):
        g1 = norm1_g[l].reshape(1, D_MODEL)
        if moe is None:
            proj = _proj_call(xt, mod[l], g1, w_in[l].astype(BF16))
        else:
            xt, proj = _proj_call(xt, mod[l], g1, w_in[l].astype(BF16), moe)
        xt, xs, lw, n_pad = _mix_call(
            l, proj, xt, mod[l], norm2_g[l].reshape(1, D_MODEL), rot, ret,
            w_ret_o, pool_w, pool_scale[l].reshape(1, POOL_W), w_pool_o, w_out, wrt, br, tri)
        block_e, n_used, src, dst = _plan(n_pad[:, :, 0])
        ys = _expert_call(l, block_e, n_used, src, dst, xs.reshape(N_TILES * SORT_R, D_MODEL),
                          w_e_gate, w_e_up, w_e_down, ys)
        moe = (lw.T, ys, mod[l])
    out = _final_call(xt, *moe, final_g.reshape(1, D_MODEL))
    return out.reshape(BATCH, SEQ, D_MODEL)
```

```python
import functools

import numpy as np
import jax
import jax.numpy as jnp
from jax import lax
from jax.experimental import pallas as pl
from jax.experimental.pallas import tpu as pltpu

D_MODEL = 1024
BATCH = 16
SEQ = 2048
DEPTH = 4
N_TOK = BATCH * SEQ

RET_HEADS = 4
RET_QK_DIM = 128
RET_V_DIM = 256
RET_QK_W = RET_HEADS * RET_QK_DIM
RET_V_W = RET_HEADS * RET_V_DIM
ROPE_BASE = 10000.0
POOL_WINDOWS = (2, 4, 8, 16)
POOL_GROUPS = 4
POOL_GROUP_DIM = 128
POOL_W = POOL_GROUPS * POOL_GROUP_DIM
POOL_HALO = 16
IN_W = 2 * RET_QK_W + 2 * RET_V_W + POOL_W + 2 * D_MODEL
OFF_Q = 0
OFF_K = OFF_Q + RET_QK_W
OFF_V = OFF_K + RET_QK_W
OFF_G = OFF_V + RET_V_W
OFF_U = OFF_G + RET_V_W
OFF_AR = OFF_U + POOL_W
OFF_AP = OFF_AR + D_MODEL

N_EXPERTS = 16
N_GROUPS = 4
EXPERTS_PER_GROUP = 4
TOP_K = 2
D_EXPERT = D_MODEL // 2
EPS = 1e-6

TILE = 512
PROJ_TN = 512
RET_C = 256
N_TILES = N_TOK // TILE
ALIGN = 16
SORT_R = -(-(TOP_K * TILE + N_EXPERTS * (ALIGN - 1)) // 128) * 128
SORT_TN = 256
EXP_BLK = 512
UNITS_PER_BLK = EXP_BLK // ALIGN
MAX_UNITS = N_TILES * ((TOP_K * TILE + N_EXPERTS * (ALIGN - 1)) // ALIGN)
N_BLOCKS = -(-MAX_UNITS // UNITS_PER_BLK) + N_EXPERTS
DUMP_UNITS = N_EXPERTS * (UNITS_PER_BLK - 1)
YS_ROWS = (N_TILES + -(-DUMP_UNITS * ALIGN // SORT_R)) * SORT_R
VMEM_LIMIT = 56 * 1024 * 1024

F32 = jnp.float32
BF16 = jnp.bfloat16
I32 = jnp.int32


def _const_spec(shape):
    zeros = (0,) * len(shape)
    return pl.BlockSpec(shape, lambda *_: zeros, pipeline_mode=pl.Buffered(1))


def _rms(xf):
    return xf * lax.rsqrt(jnp.mean(xf * xf, axis=-1, keepdims=True) + EPS)


def _dot(a, b):
    return jnp.dot(a, b, preferred_element_type=F32)


def _sigmoid(x):
    return 0.5 * jnp.tanh(0.5 * x) + 0.5


def _silu(x):
    return x * _sigmoid(x)


def _ada_kernel(c_ref, w_ref, b_ref, o_ref):
    cond = jax.nn.silu(c_ref[...]).astype(BF16)
    o_ref[0] = _dot(cond, w_ref[0].astype(BF16)) + b_ref[0]


def _ada_call(c, w_ada, b_ada):
    n_col = 6
    return pl.pallas_call(
        _ada_kernel,
        grid=(DEPTH, n_col),
        in_specs=[
            pl.BlockSpec((BATCH, D_MODEL), lambda l, j: (0, 0)),
            pl.BlockSpec((1, D_MODEL, D_MODEL), lambda l, j: (l, 0, j)),
            pl.BlockSpec((1, 1, D_MODEL), lambda l, j: (l, 0, j)),
        ],
        out_specs=pl.BlockSpec((1, BATCH, D_MODEL), lambda l, j: (l, 0, j)),
        out_shape=jax.ShapeDtypeStruct((DEPTH, BATCH, n_col * D_MODEL), F32),
        compiler_params=pltpu.CompilerParams(
            dimension_semantics=("arbitrary", "arbitrary"), vmem_limit_bytes=VMEM_LIMIT),
        name="ada_mod",
    )(c, w_ada, b_ada.reshape(DEPTH, 1, n_col * D_MODEL))


def _combine_weights(cols):
    lane = lax.broadcasted_iota(I32, (TILE, SORT_R), 1)
    l0 = cols[:, 0:1].astype(I32)
    l1 = cols[:, 1:2].astype(I32)
    pw = jnp.where(lane == l0, cols[:, 2:3], 0.0) + jnp.where(lane == l1, cols[:, 3:4], 0.0)
    return pw.astype(BF16)


def _moe_combine(x, cols, ys, gate):
    return x + gate * _dot(_combine_weights(cols), ys)


def _proj_kernel(combine, *refs):
    if combine:
        x_ref, cols_ref, ys_ref, modp_ref, mod_ref, g_ref, w_ref, xo_ref, o_ref, h_prev = refs
    else:
        x_ref, mod_ref, g_ref, w_ref, o_ref, h_prev = refs

    @pl.when(pl.program_id(0) == 0)
    def _():
        h_prev[...] = jnp.zeros_like(h_prev)

    hp = h_prev[...]
    n_chunks = IN_W // PROJ_TN

    def proj_chunks(lo, hi):
        for n in range(lo, hi):
            cols = slice(n * PROJ_TN, (n + 1) * PROJ_TN)
            o_ref[:, cols] = _dot(hp, w_ref[:, cols]).astype(BF16)

    if combine:
        pw = _combine_weights(cols_ref[...])
        proj_chunks(0, n_chunks // 2)
        x = x_ref[...] + modp_ref[0, 5:6, :] * _dot(pw, ys_ref[...])
        xo_ref[...] = x
        proj_chunks(n_chunks // 2, n_chunks)
    else:
        x = x_ref[...]
        proj_chunks(0, n_chunks)
    h = (_rms(x) * g_ref[...]) * (1.0 + mod_ref[0, 1:2, :]) + mod_ref[0, 0:1, :]
    h_prev[...] = h.astype(BF16)


def _proj_call(x, mod_l, g, w_in_l, moe=None):
    tiles_per_b = SEQ // TILE
    last = N_TILES - 1
    cur = lambda i: jnp.minimum(i, last)
    tok_in = pl.BlockSpec((TILE, D_MODEL), lambda i: (cur(i), 0))
    mod_spec = pl.BlockSpec((1, 6, D_MODEL), lambda i: (cur(i) // tiles_per_b, 0, 0))
    proj_spec = pl.BlockSpec((TILE, IN_W), lambda i: (jnp.maximum(i - 1, 0), 0))
    proj_shape = jax.ShapeDtypeStruct((N_TOK, IN_W), BF16)
    tail_specs = [mod_spec, _const_spec((1, D_MODEL)), _const_spec((D_MODEL, IN_W))]
    if moe is None:
        in_specs = [tok_in] + tail_specs
        args = (x, mod_l, g, w_in_l)
        out_specs, out_shape = proj_spec, proj_shape
    else:
        cols, ys, mod_prev = moe
        in_specs = [tok_in, pl.BlockSpec((TILE, 4), lambda i: (cur(i), 0)),
                    pl.BlockSpec((SORT_R, D_MODEL), lambda i: (cur(i), 0)), mod_spec] + tail_specs
        args = (x, cols, ys, mod_prev, mod_l, g, w_in_l)
        out_specs = [pl.BlockSpec((TILE, D_MODEL), lambda i: (i, 0)), proj_spec]
        out_shape = [jax.ShapeDtypeStruct((N_TOK + TILE, D_MODEL), F32), proj_shape]
    return pl.pallas_call(
        functools.partial(_proj_kernel, moe is not None),
        grid=(N_TILES + 1,),
        in_specs=in_specs,
        out_specs=out_specs,
        out_shape=out_shape,
        scratch_shapes=[pltpu.VMEM((TILE, D_MODEL), BF16)],
        compiler_params=pltpu.CompilerParams(
            dimension_semantics=("arbitrary",), vmem_limit_bytes=VMEM_LIMIT),
        name="in_proj",
    )(*args)


def _final_kernel(x_ref, cols_ref, ys_ref, modp_ref, g_ref, o_ref):
    x = _moe_combine(x_ref[...], cols_ref[...], ys_ref[...], modp_ref[0, 5:6, :])
    o_ref[...] = _rms(x) * g_ref[...]


def _final_call(x, cols, ys, mod_prev, final_g):
    tiles_per_b = SEQ // TILE
    tok = pl.BlockSpec((TILE, D_MODEL), lambda i: (i, 0))
    return pl.pallas_call(
        _final_kernel,
        grid=(N_TILES,),
        in_specs=[tok, pl.BlockSpec((TILE, 4), lambda i: (i, 0)),
                  pl.BlockSpec((SORT_R, D_MODEL), lambda i: (i, 0)),
                  pl.BlockSpec((1, 6, D_MODEL), lambda i: (i // tiles_per_b, 0, 0)),
                  _const_spec((1, D_MODEL))],
        out_specs=tok,
        out_shape=jax.ShapeDtypeStruct((N_TOK, D_MODEL), F32),
        compiler_params=pltpu.CompilerParams(
            dimension_semantics=("arbitrary",), vmem_limit_bytes=VMEM_LIMIT),
        name="final_norm",
    )(x, cols, ys, mod_prev, final_g)


def _top2_sum(a, b, c, d):
    hi1, lo1 = jnp.maximum(a, b), jnp.minimum(a, b)
    hi2, lo2 = jnp.maximum(c, d), jnp.minimum(c, d)
    return jnp.maximum(hi1, hi2) + jnp.maximum(jnp.minimum(hi1, hi2), jnp.maximum(lo1, lo2))


def _first_argmax(vals):
    best = vals[0]
    idx = jnp.zeros(best.shape, I32)
    for j in range(1, len(vals)):
        better = vals[j] > best
        idx = jnp.where(better, j, idx)
        best = jnp.where(better, vals[j], best)
    return idx, best


def _mix_kernel(gamma_c,
                proj_ref, x_ref, mod_ref, n2g_ref, cq_ref, sq_ref, ck_ref, sk_ref,
                dec_ref, xi_ref, zeta_ref, wro_ref, pw_ref, ps_ref, wpo_ref, wout_ref,
                wrt_ref, br_ref, tri_ref,
                xo_ref, xs_ref, lw_ref, np_ref,
                state_ref, halo_ref, gated_ref, wro_bf, pw_bf, wpo_bf, wout_bf):
    s = pl.program_id(1)
    ts = TILE

    @pl.when((pl.program_id(0) == 0) & (s == 0))
    def _():
        wro_bf[...] = wro_ref[0].astype(BF16)
        pw_bf[...] = pw_ref[0].astype(BF16)
        wpo_bf[...] = wpo_ref[0].astype(BF16)
        wout_bf[...] = wout_ref[0].astype(BF16)

    @pl.when(s == 0)
    def _():
        state_ref[...] = jnp.zeros_like(state_ref)
        halo_ref[...] = jnp.zeros_like(halo_ref)

    for j in range(ts // RET_C):
        rows = slice(j * RET_C, (j + 1) * RET_C)
        cq, sq = cq_ref[rows, :], sq_ref[rows, :]
        ck, sk = ck_ref[rows, :], sk_ref[rows, :]
        for h in range(RET_HEADS):
            qh = proj_ref[rows, OFF_Q + h * RET_QK_DIM:OFF_Q + (h + 1) * RET_QK_DIM].astype(F32)
            kh = proj_ref[rows, OFF_K + h * RET_QK_DIM:OFF_K + (h + 1) * RET_QK_DIM].astype(F32)
            vh = proj_ref[rows, OFF_V + h * RET_V_DIM:OFF_V + (h + 1) * RET_V_DIM]
            qh = qh * cq + pltpu.roll(qh, RET_QK_DIM // 2, axis=1) * sq
            kh = kh * ck + pltpu.roll(kh, RET_QK_DIM // 2, axis=1) * sk
            sc = lax.dot_general(qh.astype(BF16), kh.astype(BF16), (((1,), (1,)), ((), ())),
                                 preferred_element_type=F32) * dec_ref[h]
            st = state_ref[h]
            y = _dot(sc.astype(BF16), vh) + _dot((qh * xi_ref[h]).astype(BF16), st.astype(BF16))
            kv = lax.dot_general((kh * zeta_ref[h]).astype(BF16), vh, (((0,), (0,)), ((), ())),
                                 preferred_element_type=F32)
            state_ref[h] = st * gamma_c[h] + kv
            gh = proj_ref[rows, OFF_G + h * RET_V_DIM:OFF_G + (h + 1) * RET_V_DIM].astype(F32)
            gated_ref[rows, h * RET_V_DIM:(h + 1) * RET_V_DIM] = (_silu(gh) * _rms(y)).astype(BF16)
    y_ret = _dot(gated_ref[...], wro_bf[...])

    u = proj_ref[:, OFF_U:OFF_U + POOL_W].astype(F32)
    ext = jnp.concatenate([halo_ref[...], u], axis=0)
    halo_ref[...] = u[ts - POOL_HALO:, :]
    pos = s * ts + lax.broadcasted_iota(I32, (ts, 1), 0)
    pooled = []
    for g, win in enumerate(POOL_WINDOWS):
        cols = slice(g * POOL_GROUP_DIM, (g + 1) * POOL_GROUP_DIM)
        acc = ext[:, cols]
        shift = 1
        while shift < win:
            acc = acc + pltpu.roll(acc, shift, axis=0)
            shift *= 2
        count = jnp.minimum(pos + 1, win).astype(F32)
        mix = acc[POOL_HALO:, :] / count - u[:, cols]
        pooled.append(_dot(mix.astype(BF16), pw_bf[g]))
    y_pool = _dot((jnp.concatenate(pooled, axis=1) * ps_ref[...]).astype(BF16), wpo_bf[...])

    a_ret = proj_ref[:, OFF_AR:OFF_AR + D_MODEL].astype(F32)
    a_pool = proj_ref[:, OFF_AP:OFF_AP + D_MODEL].astype(F32)
    merged = _sigmoid(a_ret) * y_ret + _sigmoid(a_pool) * y_pool
    x_new = x_ref[...] + mod_ref[0, 2:3, :] * _dot(merged.astype(BF16), wout_bf[...])
    xo_ref[...] = x_new

    h2 = (_rms(x_new) * n2g_ref[...]) * (1.0 + mod_ref[0, 4:5, :]) + mod_ref[0, 3:4, :]
    h2b = h2.astype(BF16)

    logits = lax.dot_general(wrt_ref[...], h2b, (((1,), (1,)), ((), ())),
                             preferred_element_type=F32) + br_ref[...]
    p = jnp.exp(logits - jnp.max(logits, axis=0, keepdims=True))
    probs = p / jnp.sum(p, axis=0, keepdims=True)
    pr = [probs[e:e + 1, :] for e in range(N_EXPERTS)]
    scores = [_top2_sum(*pr[g * EXPERTS_PER_GROUP:(g + 1) * EXPERTS_PER_GROUP]) for g in range(N_GROUPS)]
    grp, _ = _first_argmax(scores)
    in_grp = []
    for jj in range(EXPERTS_PER_GROUP):
        v = pr[jj]
        for g in range(1, N_GROUPS):
            v = jnp.where(grp == g, pr[g * EXPERTS_PER_GROUP + jj], v)
        in_grp.append(v)
    i0, p0 = _first_argmax(in_grp)
    i1, p1 = _first_argmax([jnp.where(i0 == jj, -1.0, in_grp[jj]) for jj in range(EXPERTS_PER_GROUP)])
    e0 = grp * EXPERTS_PER_GROUP + i0
    e1 = grp * EXPERTS_PER_GROUP + i1
    den = p0 + p1

    erow = lax.broadcasted_iota(I32, (N_EXPERTS, ts), 0)
    oh0 = erow == e0
    oh1 = erow == e1
    ohf = jnp.where(oh0 | oh1, 1.0, 0.0)
    n_e = jnp.sum(ohf, axis=1, keepdims=True)
    n_pad = jnp.broadcast_to(jnp.floor((n_e + (ALIGN - 1.0)) * (1.0 / ALIGN)) * ALIGN, (N_EXPERTS, 128))
    np_ref[0] = n_pad
    erow128 = lax.broadcasted_iota(I32, (N_EXPERTS, 128), 0)
    run_end = n_pad
    shift = 1
    while shift < N_EXPERTS:
        run_end = run_end + jnp.where(erow128 >= shift, pltpu.roll(run_end, shift, axis=0), 0.0)
        shift *= 2
    run_start = (run_end - n_pad)[:, 0:1]
    row_of = _dot(ohf.astype(BF16), tri_ref[...]) + run_start
    l0 = jnp.sum(jnp.where(oh0, row_of, 0.0), axis=0, keepdims=True)
    l1 = jnp.sum(jnp.where(oh1, row_of, 0.0), axis=0, keepdims=True)
    lw_ref[...] = jnp.concatenate([l0, l1, p0 / den, p1 / den], axis=0)

    srow = lax.broadcasted_iota(I32, (SORT_R, ts), 0)
    perm = jnp.where((srow == l0.astype(I32)) | (srow == l1.astype(I32)), 1.0, 0.0).astype(BF16)
    for n in range(D_MODEL // SORT_TN):
        cols = slice(n * SORT_TN, (n + 1) * SORT_TN)
        xs_ref[0, :, cols] = _dot(perm, h2b[:, cols]).astype(BF16)


def _retention_tables():
    lg = np.log(1.0 - 2.0 ** (-5.0 - np.arange(RET_HEADS, dtype=np.float32))).astype(np.float32)
    idx = np.arange(RET_C, dtype=np.float32)
    rel = idx[:, None] - idx[None, :]
    causal = rel >= 0
    dec = np.where(causal[None], np.exp(lg[:, None, None] * np.where(causal, rel, 0.0)[None]), 0.0)
    xi = np.exp(lg[:, None] * (idx[None, :] + 1.0))
    zeta = np.exp(lg[:, None] * (RET_C - 1.0 - idx[None, :]))
    gamma_c = tuple(float(v) for v in np.exp(lg * RET_C).astype(np.float32))
    xi = np.broadcast_to(xi[:, :, None], (RET_HEADS, RET_C, RET_QK_DIM))
    zeta = np.broadcast_to(zeta[:, :, None], (RET_HEADS, RET_C, RET_QK_DIM))
    return (jnp.asarray(dec, F32), jnp.asarray(xi, F32), jnp.asarray(zeta, F32), gamma_c)


def _rotary_tables():
    freqs = ROPE_BASE ** (-jnp.arange(0, RET_QK_DIM, 2, dtype=F32) / RET_QK_DIM)
    ang = jnp.arange(SEQ, dtype=F32)[:, None] * freqs[None, :]
    cos, sin = jnp.cos(ang), jnp.sin(ang)
    cos2 = jnp.concatenate([cos, cos], axis=1)
    sin2 = jnp.concatenate([-sin, sin], axis=1)
    k_scale = RET_QK_DIM ** -0.5
    return cos2, sin2, cos2 * k_scale, sin2 * k_scale


def _mix_call(layer, proj, x, mod_l, n2g, rot, ret, wro, pw, ps, wpo, wout, wrt, br, tri):
    dec, xi, zeta, gamma_c = ret

    def layer_spec(shape):
        zeros = (0,) * len(shape)
        return pl.BlockSpec((1,) + shape, lambda b, s: (layer,) + zeros, pipeline_mode=pl.Buffered(1))

    ts = TILE
    n_s = SEQ // ts
    tok = lambda b, s: (b * n_s + s, 0)
    seq_tab = pl.BlockSpec((ts, RET_QK_DIM), lambda b, s: (s, 0))
    return pl.pallas_call(
        functools.partial(_mix_kernel, gamma_c),
        grid=(BATCH, n_s),
        in_specs=[
            pl.BlockSpec((ts, IN_W), tok),
            pl.BlockSpec((ts, D_MODEL), tok),
            pl.BlockSpec((1, 6, D_MODEL), lambda b, s: (b, 0, 0)),
            _const_spec((1, D_MODEL)),
            seq_tab, seq_tab, seq_tab, seq_tab,
            _const_spec((RET_HEADS, RET_C, RET_C)),
            _const_spec((RET_HEADS, RET_C, RET_QK_DIM)),
            _const_spec((RET_HEADS, RET_C, RET_QK_DIM)),
            layer_spec((RET_V_W, D_MODEL)),
            layer_spec((POOL_GROUPS, POOL_GROUP_DIM, POOL_GROUP_DIM)),
            _const_spec((1, POOL_W)),
            layer_spec((POOL_W, D_MODEL)),
            layer_spec((D_MODEL, D_MODEL)),
            _const_spec((N_EXPERTS, D_MODEL)),
            _const_spec((N_EXPERTS, 1)),
            _const_spec((ts, ts)),
        ],
        out_specs=[
            pl.BlockSpec((ts, D_MODEL), tok),
            pl.BlockSpec((1, SORT_R, D_MODEL), lambda b, s: (b * n_s + s, 0, 0)),
            pl.BlockSpec((4, ts), lambda b, s: (0, b * n_s + s)),
            pl.BlockSpec((1, N_EXPERTS, 128), lambda b, s: (b * n_s + s, 0, 0)),
        ],
        out_shape=[
            jax.ShapeDtypeStruct((N_TOK, D_MODEL), F32),
            jax.ShapeDtypeStruct((N_TILES, SORT_R, D_MODEL), BF16),
            jax.ShapeDtypeStruct((4, N_TOK), F32),
            jax.ShapeDtypeStruct((N_TILES, N_EXPERTS, 128), F32),
        ],
        scratch_shapes=[
            pltpu.VMEM((RET_HEADS, RET_QK_DIM, RET_V_DIM), F32),
            pltpu.VMEM((POOL_HALO, POOL_W), F32),
            pltpu.VMEM((ts, RET_V_W), BF16),
            pltpu.VMEM((RET_V_W, D_MODEL), BF16),
            pltpu.VMEM((POOL_GROUPS, POOL_GROUP_DIM, POOL_GROUP_DIM), BF16),
            pltpu.VMEM((POOL_W, D_MODEL), BF16),
            pltpu.VMEM((D_MODEL, D_MODEL), BF16),
        ],
        compiler_params=pltpu.CompilerParams(
            dimension_semantics=("arbitrary", "arbitrary"), vmem_limit_bytes=VMEM_LIMIT),
        name="mixer_route",
    )(proj, x, mod_l, n2g, *rot, dec, xi, zeta, wro, pw, ps, wpo, wout, wrt, br, tri)


def _plan(n_pad):
    n_pad = n_pad.astype(I32)
    units = (n_pad // ALIGN).T
    run_row = (jnp.cumsum(n_pad, axis=1) - n_pad).T
    unit_start = jnp.cumsum(units, axis=1) - units
    total = units.sum(axis=1)
    n_blk = (total + UNITS_PER_BLK - 1) // UNITS_PER_BLK
    blk_end = jnp.cumsum(n_blk)
    blk_start = blk_end - n_blk
    n_used = blk_end[-1:]
    b = jnp.arange(N_BLOCKS + 1, dtype=I32)
    e_b = jnp.minimum((b[:, None] >= blk_end[None, :]).sum(axis=1), N_EXPERTS - 1).astype(I32)
    oh_e = (e_b[:, None] == jnp.arange(N_EXPERTS, dtype=I32)[None, :]).astype(I32)
    pick = lambda tab: (oh_e[:, :, None] * tab[None, :, :]).sum(axis=1)
    unit_start_b, units_b, run_row_b = pick(unit_start), pick(units), pick(run_row)
    total_b = (oh_e * total[None, :]).sum(axis=1)
    first_q = (b - (oh_e * blk_start[None, :]).sum(axis=1)) * UNITS_PER_BLK
    n_valid = jnp.where(b < n_used[0], jnp.clip(total_b - first_q, 0, UNITS_PER_BLK), 0).astype(I32)
    q = first_q[:, None] + jnp.arange(UNITS_PER_BLK, dtype=I32)[None, :]
    tile = jnp.minimum(((unit_start_b + units_b)[:, None, :] <= q[:, :, None]).sum(axis=2), N_TILES - 1)
    oh_t = (tile[:, :, None] == jnp.arange(N_TILES, dtype=I32)[None, None, :]).astype(I32)
    base = jnp.arange(N_TILES, dtype=I32)[None, :] * SORT_R + run_row_b - unit_start_b * ALIGN
    row = q * ALIGN + (oh_t * base[:, None, :]).sum(axis=2)
    m = jnp.arange(UNITS_PER_BLK, dtype=I32)[None, :]
    valid = m < n_valid[:, None]
    pad_rank = jnp.clip(m - n_valid[:, None], 0, UNITS_PER_BLK - 2)
    dump = N_TILES * SORT_R + (e_b[:, None] * (UNITS_PER_BLK - 1) + pad_rank) * ALIGN
    src = jnp.where(valid, row, 0).astype(I32).reshape(-1)
    dst = jnp.where(valid, row, dump).astype(I32).reshape(-1)
    return e_b, n_used.astype(I32), src, dst


def _expert_kernel(be_ref, nu_ref, src_ref, dst_ref,
                   xs_hbm, wg_ref, wu_ref, wd_ref, ys_in_hbm, ys_hbm,
                   xbuf, ybuf, wg_bf, wu_bf, wd_bf, sem_in, sem_out):
    del ys_in_hbm
    b = pl.program_id(0)
    n_used = nu_ref[0]
    slot = b % 2

    def start_in(blk, sl):
        rows = [pl.multiple_of(src_ref[blk * UNITS_PER_BLK + m], ALIGN) for m in range(UNITS_PER_BLK)]
        for m in range(UNITS_PER_BLK):
            pltpu.make_async_copy(xs_hbm.at[pl.ds(rows[m], ALIGN), :],
                                  xbuf.at[sl, pl.ds(m * ALIGN, ALIGN), :], sem_in.at[sl]).start()

    def start_out(blk, sl):
        rows = [pl.multiple_of(dst_ref[blk * UNITS_PER_BLK + m], ALIGN) for m in range(UNITS_PER_BLK)]
        for m in range(UNITS_PER_BLK):
            pltpu.make_async_copy(ybuf.at[sl, pl.ds(m * ALIGN, ALIGN), :],
                                  ys_hbm.at[pl.ds(rows[m], ALIGN), :], sem_out.at[sl]).start()

    def wait_in(sl):
        pltpu.make_async_copy(xs_hbm.at[pl.ds(0, EXP_BLK), :], xbuf.at[sl], sem_in.at[sl]).wait()

    def wait_out(sl):
        pltpu.make_async_copy(ybuf.at[sl], ys_hbm.at[pl.ds(0, EXP_BLK), :], sem_out.at[sl]).wait()

    @pl.when(b == 0)
    def _():
        start_in(0, 0)

    @pl.when(b < n_used)
    def _():
        wait_in(slot)

        @pl.when(b >= 2)
        def _():
            wait_out(slot)

        @pl.when((b == 0) | (be_ref[b] != be_ref[jnp.maximum(b - 1, 0)]))
        def _():
            wg_bf[...] = wg_ref[0, 0].astype(BF16)
            wu_bf[...] = wu_ref[0, 0].astype(BF16)
            wd_bf[...] = wd_ref[0, 0].astype(BF16)

        start_in(b + 1, 1 - slot)
        xb = xbuf[slot]
        a = _dot(xb, wg_bf[...])
        u = _dot(xb, wu_bf[...])
        ybuf[slot] = _dot((_silu(a) * u).astype(BF16), wd_bf[...]).astype(BF16)
        start_out(b, slot)

        @pl.when(b == n_used - 1)
        def _():
            wait_in(1 - slot)

            @pl.when(b >= 1)
            def _():
                wait_out(1 - slot)
            wait_out(slot)


def _expert_call(layer, block_e, n_used, src, dst, xs, wg, wu, wd, ys):
    def w_map(i, be, nu, sr, ds):
        return (layer, be[i], 0, 0)

    grid_spec = pltpu.PrefetchScalarGridSpec(
        num_scalar_prefetch=4,
        grid=(N_BLOCKS,),
        in_specs=[
            pl.BlockSpec(memory_space=pl.ANY),
            pl.BlockSpec((1, 1, D_MODEL, D_EXPERT), w_map),
            pl.BlockSpec((1, 1, D_MODEL, D_EXPERT), w_map),
            pl.BlockSpec((1, 1, D_EXPERT, D_MODEL), w_map),
            pl.BlockSpec(memory_space=pl.ANY),
        ],
        out_specs=pl.BlockSpec(memory_space=pl.ANY),
        scratch_shapes=[
            pltpu.VMEM((2, EXP_BLK, D_MODEL), BF16),
            pltpu.VMEM((2, EXP_BLK, D_MODEL), BF16),
            pltpu.VMEM((D_MODEL, D_EXPERT), BF16),
            pltpu.VMEM((D_MODEL, D_EXPERT), BF16),
            pltpu.VMEM((D_EXPERT, D_MODEL), BF16),
            pltpu.SemaphoreType.DMA((2,)),
            pltpu.SemaphoreType.DMA((2,)),
        ],
    )
    return pl.pallas_call(
        _expert_kernel,
        grid_spec=grid_spec,
        out_shape=jax.ShapeDtypeStruct((YS_ROWS, D_MODEL), BF16),
        input_output_aliases={8: 0},
        compiler_params=pltpu.CompilerParams(
            dimension_semantics=("arbitrary",), vmem_limit_bytes=VMEM_LIMIT, has_side_effects=True),
        name="moe_experts",
    )(block_e, n_used, src, dst, xs, wg, wu, wd, ys)


def kernel(x, c, w_ada, b_ada, norm1_g, norm2_g, w_in, w_ret_o, pool_w, pool_scale, w_pool_o, w_out,
           w_router, b_router, w_e_gate, w_e_up, w_e_down, final_g):
    assert x.shape == (BATCH, SEQ, D_MODEL) and w_in.shape == (DEPTH, D_MODEL, IN_W)
    xt = x.reshape(N_TOK, D_MODEL)
    mod = _ada_call(c, w_ada, b_ada).reshape(DEPTH, BATCH, 6, D_MODEL)

    rot = _rotary_tables()
    ret = _retention_tables()
    tri = (jnp.arange(TILE)[:, None] < jnp.arange(TILE)[None, :]).astype(BF16)
    wrt = w_router.T.astype(BF16)
    br = b_router.astype(F32).reshape(N_EXPERTS, 1)
    ys = jnp.zeros((YS_ROWS, D_MODEL), BF16)

    moe = None
    for l in range(DEPTH):
        g1 = norm1_g[l].reshape(1, D_MODEL)
        if moe is None:
            proj = _proj_call(xt, mod[l], g1, w_in[l].astype(BF16))
        else:
            xt, proj = _proj_call(xt, mod[l], g1, w_in[l].astype(BF16), moe)
        xt, xs, lw, n_pad = _mix_call(
            l, proj, xt, mod[l], norm2_g[l].reshape(1, D_MODEL), rot, ret,
            w_ret_o, pool_w, pool_scale[l].reshape(1, POOL_W), w_pool_o, w_out, wrt, br, tri)
        block_e, n_used, src, dst = _plan(n_pad[:, :, 0])
        ys = _expert_call(l, block_e, n_used, src, dst, xs.reshape(N_TILES * SORT_R, D_MODEL),
                          w_e_gate, w_e_up, w_e_down, ys)
        moe = (lw.T, ys, mod[l])
    out = _final_call(xt, *moe, final_g.reshape(1, D_MODEL))
    return out.reshape(BATCH, SEQ, D_MODEL)
```

```python
import functools

import numpy as np
import jax
import jax.numpy as jnp
from jax import lax
from jax.experimental import pallas as pl
from jax.experimental.pallas import tpu as pltpu

D_MODEL = 1024
BATCH = 16
SEQ = 2048
DEPTH = 4
N_TOK = BATCH * SEQ

RET_HEADS = 4
RET_QK_DIM = 128
RET_V_DIM = 256
RET_QK_W = RET_HEADS * RET_QK_DIM
RET_V_W = RET_HEADS * RET_V_DIM
ROPE_BASE = 10000.0
POOL_WINDOWS = (2, 4, 8, 16)
POOL_GROUPS = 4
POOL_GROUP_DIM = 128
POOL_W = POOL_GROUPS * POOL_GROUP_DIM
POOL_HALO = 16
IN_W = 2 * RET_QK_W + 2 * RET_V_W + POOL_W + 2 * D_MODEL
OFF_Q = 0
OFF_K = OFF_Q + RET_QK_W
OFF_V = OFF_K + RET_QK_W
OFF_G = OFF_V + RET_V_W
OFF_U = OFF_G + RET_V_W
OFF_AR = OFF_U + POOL_W
OFF_AP = OFF_AR + D_MODEL

N_EXPERTS = 16
N_GROUPS = 4
EXPERTS_PER_GROUP = 4
TOP_K = 2
D_EXPERT = D_MODEL // 2
EPS = 1e-6

TILE = 512
PROJ_TN = 512
RET_C = 256
N_TILES = N_TOK // TILE
ALIGN = 16
SORT_R = -(-(TOP_K * TILE + N_EXPERTS * (ALIGN - 1)) // 128) * 128
SORT_TN = 256
EXP_BLK = 1024
UNITS_PER_BLK = EXP_BLK // ALIGN
MAX_UNITS = N_TILES * ((TOP_K * TILE + N_EXPERTS * (ALIGN - 1)) // ALIGN)
N_BLOCKS = -(-MAX_UNITS // UNITS_PER_BLK) + N_EXPERTS
DUMP_UNITS = N_EXPERTS * (UNITS_PER_BLK - 1)
YS_ROWS = (N_TILES + -(-DUMP_UNITS * ALIGN // SORT_R)) * SORT_R
VMEM_LIMIT = 56 * 1024 * 1024

F32 = jnp.float32
BF16 = jnp.bfloat16
I32 = jnp.int32


def _const_spec(shape):
    zeros = (0,) * len(shape)
    return pl.BlockSpec(shape, lambda *_: zeros, pipeline_mode=pl.Buffered(1))


def _rms(xf):
    return xf * lax.rsqrt(jnp.mean(xf * xf, axis=-1, keepdims=True) + EPS)


def _dot(a, b):
    return jnp.dot(a, b, preferred_element_type=F32)


def _sigmoid(x):
    return 0.5 * jnp.tanh(0.5 * x) + 0.5


def _silu(x):
    return x * _sigmoid(x)


def _ada_kernel(c_ref, w_ref, b_ref, o_ref):
    cond = jax.nn.silu(c_ref[...]).astype(BF16)
    o_ref[0] = _dot(cond, w_ref[0].astype(BF16)) + b_ref[0]


def _ada_call(c, w_ada, b_ada):
    n_col = 6
    return pl.pallas_call(
        _ada_kernel,
        grid=(DEPTH, n_col),
        in_specs=[
            pl.BlockSpec((BATCH, D_MODEL), lambda l, j: (0, 0)),
            pl.BlockSpec((1, D_MODEL, D_MODEL), lambda l, j: (l, 0, j)),
            pl.BlockSpec((1, 1, D_MODEL), lambda l, j: (l, 0, j)),
        ],
        out_specs=pl.BlockSpec((1, BATCH, D_MODEL), lambda l, j: (l, 0, j)),
        out_shape=jax.ShapeDtypeStruct((DEPTH, BATCH, n_col * D_MODEL), F32),
        compiler_params=pltpu.CompilerParams(
            dimension_semantics=("arbitrary", "arbitrary"), vmem_limit_bytes=VMEM_LIMIT),
        name="ada_mod",
    )(c, w_ada, b_ada.reshape(DEPTH, 1, n_col * D_MODEL))


def _combine_weights(cols):
    lane = lax.broadcasted_iota(I32, (TILE, SORT_R), 1)
    l0 = cols[:, 0:1].astype(I32)
    l1 = cols[:, 1:2].astype(I32)
    pw = jnp.where(lane == l0, cols[:, 2:3], 0.0) + jnp.where(lane == l1, cols[:, 3:4], 0.0)
    return pw.astype(BF16)


def _moe_combine(x, cols, ys, gate):
    return x + gate * _dot(_combine_weights(cols), ys)


def _proj_kernel(combine, *refs):
    if combine:
        x_ref, cols_ref, ys_ref, modp_ref, mod_ref, g_ref, w_ref, xo_ref, o_ref, h_prev = refs
    else:
        x_ref, mod_ref, g_ref, w_ref, o_ref, h_prev = refs

    @pl.when(pl.program_id(0) == 0)
    def _():
        h_prev[...] = jnp.zeros_like(h_prev)

    hp = h_prev[...]
    n_chunks = IN_W // PROJ_TN

    def proj_chunks(lo, hi):
        for n in range(lo, hi):
            cols = slice(n * PROJ_TN, (n + 1) * PROJ_TN)
            o_ref[:, cols] = _dot(hp, w_ref[:, cols]).astype(BF16)

    if combine:
        pw = _combine_weights(cols_ref[...])
        proj_chunks(0, n_chunks // 2)
        x = x_ref[...] + modp_ref[0, 5:6, :] * _dot(pw, ys_ref[...])
        xo_ref[...] = x
        proj_chunks(n_chunks // 2, n_chunks)
    else:
        x = x_ref[...]
        proj_chunks(0, n_chunks)
    h = (_rms(x) * g_ref[...]) * (1.0 + mod_ref[0, 1:2, :]) + mod_ref[0, 0:1, :]
    h_prev[...] = h.astype(BF16)


def _proj_call(x, mod_l, g, w_in_l, moe=None):
    tiles_per_b = SEQ // TILE
    last = N_TILES - 1
    cur = lambda i: jnp.minimum(i, last)
    tok_in = pl.BlockSpec((TILE, D_MODEL), lambda i: (cur(i), 0))
    mod_spec = pl.BlockSpec((1, 6, D_MODEL), lambda i: (cur(i) // tiles_per_b, 0, 0))
    proj_spec = pl.BlockSpec((TILE, IN_W), lambda i: (jnp.maximum(i - 1, 0), 0))
    proj_shape = jax.ShapeDtypeStruct((N_TOK, IN_W), BF16)
    tail_specs = [mod_spec, _const_spec((1, D_MODEL)), _const_spec((D_MODEL, IN_W))]
    if moe is None:
        in_specs = [tok_in] + tail_specs
        args = (x, mod_l, g, w_in_l)
        out_specs, out_shape = proj_spec, proj_shape
    else:
        cols, ys, mod_prev = moe
        in_specs = [tok_in, pl.BlockSpec((TILE, 4), lambda i: (cur(i), 0)),
                    pl.BlockSpec((SORT_R, D_MODEL), lambda i: (cur(i), 0)), mod_spec] + tail_specs
        args = (x, cols, ys, mod_prev, mod_l, g, w_in_l)
        out_specs = [pl.BlockSpec((TILE, D_MODEL), lambda i: (i, 0)), proj_spec]
        out_shape = [jax.ShapeDtypeStruct((N_TOK + TILE, D_MODEL), F32), proj_shape]
    return pl.pallas_call(
        functools.partial(_proj_kernel, moe is not None),
        grid=(N_TILES + 1,),
        in_specs=in_specs,
        out_specs=out_specs,
        out_shape=out_shape,
        scratch_shapes=[pltpu.VMEM((TILE, D_MODEL), BF16)],
        compiler_params=pltpu.CompilerParams(
            dimension_semantics=("arbitrary",), vmem_limit_bytes=VMEM_LIMIT),
        name="in_proj",
    )(*args)


def _final_kernel(x_ref, cols_ref, ys_ref, modp_ref, g_ref, o_ref):
    x = _moe_combine(x_ref[...], cols_ref[...], ys_ref[...], modp_ref[0, 5:6, :])
    o_ref[...] = _rms(x) * g_ref[...]


def _final_call(x, cols, ys, mod_prev, final_g):
    tiles_per_b = SEQ // TILE
    tok = pl.BlockSpec((TILE, D_MODEL), lambda i: (i, 0))
    return pl.pallas_call(
        _final_kernel,
        grid=(N_TILES,),
        in_specs=[tok, pl.BlockSpec((TILE, 4), lambda i: (i, 0)),
                  pl.BlockSpec((SORT_R, D_MODEL), lambda i: (i, 0)),
                  pl.BlockSpec((1, 6, D_MODEL), lambda i: (i // tiles_per_b, 0, 0)),
                  _const_spec((1, D_MODEL))],
        out_specs=tok,
        out_shape=jax.ShapeDtypeStruct((N_TOK, D_MODEL), F32),
        compiler_params=pltpu.CompilerParams(
            dimension_semantics=("arbitrary",), vmem_limit_bytes=VMEM_LIMIT),
        name="final_norm",
    )(x, cols, ys, mod_prev, final_g)


def _top2_sum(a, b, c, d):
    hi1, lo1 = jnp.maximum(a, b), jnp.minimum(a, b)
    hi2, lo2 = jnp.maximum(c, d), jnp.minimum(c, d)
    return jnp.maximum(hi1, hi2) + jnp.maximum(jnp.minimum(hi1, hi2), jnp.maximum(lo1, lo2))


def _first_argmax(vals):
    best = vals[0]
    idx = jnp.zeros(best.shape, I32)
    for j in range(1, len(vals)):
        better = vals[j] > best
        idx = jnp.where(better, j, idx)
        best = jnp.where(better, vals[j], best)
    return idx, best


def _mix_kernel(gamma_c,
                proj_ref, x_ref, mod_ref, n2g_ref, cq_ref, sq_ref, ck_ref, sk_ref,
                dec_ref, xi_ref, zeta_ref, wro_ref, pw_ref, ps_ref, wpo_ref, wout_ref,
                wrt_ref, br_ref, tri_ref,
                xo_ref, xs_ref, lw_ref, np_ref,
                state_ref, halo_ref, gated_ref, wro_bf, pw_bf, wpo_bf, wout_bf):
    s = pl.program_id(1)
    ts = TILE

    @pl.when((pl.program_id(0) == 0) & (s == 0))
    def _():
        wro_bf[...] = wro_ref[0].astype(BF16)
        pw_bf[...] = pw_ref[0].astype(BF16)
        wpo_bf[...] = wpo_ref[0].astype(BF16)
        wout_bf[...] = wout_ref[0].astype(BF16)

    @pl.when(s == 0)
    def _():
        state_ref[...] = jnp.zeros_like(state_ref)
        halo_ref[...] = jnp.zeros_like(halo_ref)

    for j in range(ts // RET_C):
        rows = slice(j * RET_C, (j + 1) * RET_C)
        cq, sq = cq_ref[rows, :], sq_ref[rows, :]
        ck, sk = ck_ref[rows, :], sk_ref[rows, :]
        for h in range(RET_HEADS):
            qh = proj_ref[rows, OFF_Q + h * RET_QK_DIM:OFF_Q + (h + 1) * RET_QK_DIM].astype(F32)
            kh = proj_ref[rows, OFF_K + h * RET_QK_DIM:OFF_K + (h + 1) * RET_QK_DIM].astype(F32)
            vh = proj_ref[rows, OFF_V + h * RET_V_DIM:OFF_V + (h + 1) * RET_V_DIM]
            qh = qh * cq + pltpu.roll(qh, RET_QK_DIM // 2, axis=1) * sq
            kh = kh * ck + pltpu.roll(kh, RET_QK_DIM // 2, axis=1) * sk
            sc = lax.dot_general(qh.astype(BF16), kh.astype(BF16), (((1,), (1,)), ((), ())),
                                 preferred_element_type=F32) * dec_ref[h]
            st = state_ref[h]
            y = _dot(sc.astype(BF16), vh) + _dot((qh * xi_ref[h]).astype(BF16), st.astype(BF16))
            kv = lax.dot_general((kh * zeta_ref[h]).astype(BF16), vh, (((0,), (0,)), ((), ())),
                                 preferred_element_type=F32)
            state_ref[h] = st * gamma_c[h] + kv
            gh = proj_ref[rows, OFF_G + h * RET_V_DIM:OFF_G + (h + 1) * RET_V_DIM].astype(F32)
            gated_ref[rows, h * RET_V_DIM:(h + 1) * RET_V_DIM] = (_silu(gh) * _rms(y)).astype(BF16)
    y_ret = _dot(gated_ref[...], wro_bf[...])

    u = proj_ref[:, OFF_U:OFF_U + POOL_W].astype(F32)
    ext = jnp.concatenate([halo_ref[...], u], axis=0)
    halo_ref[...] = u[ts - POOL_HALO:, :]
    pos = s * ts + lax.broadcasted_iota(I32, (ts, 1), 0)
    pooled = []
    for g, win in enumerate(POOL_WINDOWS):
        cols = slice(g * POOL_GROUP_DIM, (g + 1) * POOL_GROUP_DIM)
        acc = ext[:, cols]
        shift = 1
        while shift < win:
            acc = acc + pltpu.roll(acc, shift, axis=0)
            shift *= 2
        count = jnp.minimum(pos + 1, win).astype(F32)
        mix = acc[POOL_HALO:, :] / count - u[:, cols]
        pooled.append(_dot(mix.astype(BF16), pw_bf[g]))
    y_pool = _dot((jnp.concatenate(pooled, axis=1) * ps_ref[...]).astype(BF16), wpo_bf[...])

    a_ret = proj_ref[:, OFF_AR:OFF_AR + D_MODEL].astype(F32)
    a_pool = proj_ref[:, OFF_AP:OFF_AP + D_MODEL].astype(F32)
    merged = _sigmoid(a_ret) * y_ret + _sigmoid(a_pool) * y_pool
    x_new = x_ref[...] + mod_ref[0, 2:3, :] * _dot(merged.astype(BF16), wout_bf[...])
    xo_ref[...] = x_new

    h2 = (_rms(x_new) * n2g_ref[...]) * (1.0 + mod_ref[0, 4:5, :]) + mod_ref[0, 3:4, :]
    h2b = h2.astype(BF16)

    logits = lax.dot_general(wrt_ref[...], h2b, (((1,), (1,)), ((), ())),
                             preferred_element_type=F32) + br_ref[...]
    p = jnp.exp(logits - jnp.max(logits, axis=0, keepdims=True))
    probs = p / jnp.sum(p, axis=0, keepdims=True)
    pr = [probs[e:e + 1, :] for e in range(N_EXPERTS)]
    scores = [_top2_sum(*pr[g * EXPERTS_PER_GROUP:(g + 1) * EXPERTS_PER_GROUP]) for g in range(N_GROUPS)]
    grp, _ = _first_argmax(scores)
    in_grp = []
    for jj in range(EXPERTS_PER_GROUP):
        v = pr[jj]
        for g in range(1, N_GROUPS):
            v = jnp.where(grp == g, pr[g * EXPERTS_PER_GROUP + jj], v)
        in_grp.append(v)
    i0, p0 = _first_argmax(in_grp)
    i1, p1 = _first_argmax([jnp.where(i0 == jj, -1.0, in_grp[jj]) for jj in range(EXPERTS_PER_GROUP)])
    e0 = grp * EXPERTS_PER_GROUP + i0
    e1 = grp * EXPERTS_PER_GROUP + i1
    den = p0 + p1

    erow = lax.broadcasted_iota(I32, (N_EXPERTS, ts), 0)
    oh0 = erow == e0
    oh1 = erow == e1
    ohf = jnp.where(oh0 | oh1, 1.0, 0.0)
    n_e = jnp.sum(ohf, axis=1, keepdims=True)
    n_pad = jnp.broadcast_to(jnp.floor((n_e + (ALIGN - 1.0)) * (1.0 / ALIGN)) * ALIGN, (N_EXPERTS, 128))
    np_ref[0] = n_pad
    erow128 = lax.broadcasted_iota(I32, (N_EXPERTS, 128), 0)
    run_end = n_pad
    shift = 1
    while shift < N_EXPERTS:
        run_end = run_end + jnp.where(erow128 >= shift, pltpu.roll(run_end, shift, axis=0), 0.0)
        shift *= 2
    run_start = (run_end - n_pad)[:, 0:1]
    row_of = _dot(ohf.astype(BF16), tri_ref[...]) + run_start
    l0 = jnp.sum(jnp.where(oh0, row_of, 0.0), axis=0, keepdims=True)
    l1 = jnp.sum(jnp.where(oh1, row_of, 0.0), axis=0, keepdims=True)
    lw_ref[...] = jnp.concatenate([l0, l1, p0 / den, p1 / den], axis=0)

    srow = lax.broadcasted_iota(I32, (SORT_R, ts), 0)
    perm = jnp.where((srow == l0.astype(I32)) | (srow == l1.astype(I32)), 1.0, 0.0).astype(BF16)
    for n in range(D_MODEL // SORT_TN):
        cols = slice(n * SORT_TN, (n + 1) * SORT_TN)
        xs_ref[0, :, cols] = _dot(perm, h2b[:, cols]).astype(BF16)


def _retention_tables():
    lg = np.log(1.0 - 2.0 ** (-5.0 - np.arange(RET_HEADS, dtype=np.float32))).astype(np.float32)
    idx = np.arange(RET_C, dtype=np.float32)
    rel = idx[:, None] - idx[None, :]
    causal = rel >= 0
    dec = np.where(causal[None], np.exp(lg[:, None, None] * np.where(causal, rel, 0.0)[None]), 0.0)
    xi = np.exp(lg[:, None] * (idx[None, :] + 1.0))
    zeta = np.exp(lg[:, None] * (RET_C - 1.0 - idx[None, :]))
    gamma_c = tuple(float(v) for v in np.exp(lg * RET_C).astype(np.float32))
    xi = np.broadcast_to(xi[:, :, None], (RET_HEADS, RET_C, RET_QK_DIM))
    zeta = np.broadcast_to(zeta[:, :, None], (RET_HEADS, RET_C, RET_QK_DIM))
    return (jnp.asarray(dec, F32), jnp.asarray(xi, F32), jnp.asarray(zeta, F32), gamma_c)


def _rotary_tables():
    freqs = ROPE_BASE ** (-jnp.arange(0, RET_QK_DIM, 2, dtype=F32) / RET_QK_DIM)
    ang = jnp.arange(SEQ, dtype=F32)[:, None] * freqs[None, :]
    cos, sin = jnp.cos(ang), jnp.sin(ang)
    cos2 = jnp.concatenate([cos, cos], axis=1)
    sin2 = jnp.concatenate([-sin, sin], axis=1)
    k_scale = RET_QK_DIM ** -0.5
    return cos2, sin2, cos2 * k_scale, sin2 * k_scale


def _mix_call(layer, proj, x, mod_l, n2g, rot, ret, wro, pw, ps, wpo, wout, wrt, br, tri):
    dec, xi, zeta, gamma_c = ret

    def layer_spec(shape):
        zeros = (0,) * len(shape)
        return pl.BlockSpec((1,) + shape, lambda b, s: (layer,) + zeros, pipeline_mode=pl.Buffered(1))

    ts = TILE
    n_s = SEQ // ts
    tok = lambda b, s: (b * n_s + s, 0)
    seq_tab = pl.BlockSpec((ts, RET_QK_DIM), lambda b, s: (s, 0))
    return pl.pallas_call(
        functools.partial(_mix_kernel, gamma_c),
        grid=(BATCH, n_s),
        in_specs=[
            pl.BlockSpec((ts, IN_W), tok),
            pl.BlockSpec((ts, D_MODEL), tok),
            pl.BlockSpec((1, 6, D_MODEL), lambda b, s: (b, 0, 0)),
            _const_spec((1, D_MODEL)),
            seq_tab, seq_tab, seq_tab, seq_tab,
            _const_spec((RET_HEADS, RET_C, RET_C)),
            _const_spec((RET_HEADS, RET_C, RET_QK_DIM)),
            _const_spec((RET_HEADS, RET_C, RET_QK_DIM)),
            layer_spec((RET_V_W, D_MODEL)),
            layer_spec((POOL_GROUPS, POOL_GROUP_DIM, POOL_GROUP_DIM)),
            _const_spec((1, POOL_W)),
            layer_spec((POOL_W, D_MODEL)),
            layer_spec((D_MODEL, D_MODEL)),
            _const_spec((N_EXPERTS, D_MODEL)),
            _const_spec((N_EXPERTS, 1)),
            _const_spec((ts, ts)),
        ],
        out_specs=[
            pl.BlockSpec((ts, D_MODEL), tok),
            pl.BlockSpec((1, SORT_R, D_MODEL), lambda b, s: (b * n_s + s, 0, 0)),
            pl.BlockSpec((4, ts), lambda b, s: (0, b * n_s + s)),
            pl.BlockSpec((1, N_EXPERTS, 128), lambda b, s: (b * n_s + s, 0, 0)),
        ],
        out_shape=[
            jax.ShapeDtypeStruct((N_TOK, D_MODEL), F32),
            jax.ShapeDtypeStruct((N_TILES, SORT_R, D_MODEL), BF16),
            jax.ShapeDtypeStruct((4, N_TOK), F32),
            jax.ShapeDtypeStruct((N_TILES, N_EXPERTS, 128), F32),
        ],
        scratch_shapes=[
            pltpu.VMEM((RET_HEADS, RET_QK_DIM, RET_V_DIM), F32),
            pltpu.VMEM((POOL_HALO, POOL_W), F32),
            pltpu.VMEM((ts, RET_V_W), BF16),
            pltpu.VMEM((RET_V_W, D_MODEL), BF16),
            pltpu.VMEM((POOL_GROUPS, POOL_GROUP_DIM, POOL_GROUP_DIM), BF16),
            pltpu.VMEM((POOL_W, D_MODEL), BF16),
            pltpu.VMEM((D_MODEL, D_MODEL), BF16),
        ],
        compiler_params=pltpu.CompilerParams(
            dimension_semantics=("arbitrary", "arbitrary"), vmem_limit_bytes=VMEM_LIMIT),
        name="mixer_route",
    )(proj, x, mod_l, n2g, *rot, dec, xi, zeta, wro, pw, ps, wpo, wout, wrt, br, tri)


def _plan(n_pad):
    n_pad = n_pad.astype(I32)
    units = (n_pad // ALIGN).T
    run_row = (jnp.cumsum(n_pad, axis=1) - n_pad).T
    unit_start = jnp.cumsum(units, axis=1) - units
    total = units.sum(axis=1)
    n_blk = (total + UNITS_PER_BLK - 1) // UNITS_PER_BLK
    blk_end = jnp.cumsum(n_blk)
    blk_start = blk_end - n_blk
    n_used = blk_end[-1:]
    b = jnp.arange(N_BLOCKS + 1, dtype=I32)
    e_b = jnp.minimum((b[:, None] >= blk_end[None, :]).sum(axis=1), N_EXPERTS - 1).astype(I32)
    oh_e = (e_b[:, None] == jnp.arange(N_EXPERTS, dtype=I32)[None, :]).astype(I32)
    pick = lambda tab: (oh_e[:, :, None] * tab[None, :, :]).sum(axis=1)
    unit_start_b, units_b, run_row_b = pick(unit_start), pick(units), pick(run_row)
    total_b = (oh_e * total[None, :]).sum(axis=1)
    first_q = (b - (oh_e * blk_start[None, :]).sum(axis=1)) * UNITS_PER_BLK
    n_valid = jnp.where(b < n_used[0], jnp.clip(total_b - first_q, 0, UNITS_PER_BLK), 0).astype(I32)
    q = first_q[:, None] + jnp.arange(UNITS_PER_BLK, dtype=I32)[None, :]
    tile = jnp.minimum(((unit_start_b + units_b)[:, None, :] <= q[:, :, None]).sum(axis=2), N_TILES - 1)
    oh_t = (tile[:, :, None] == jnp.arange(N_TILES, dtype=I32)[None, None, :]).astype(I32)
    base = jnp.arange(N_TILES, dtype=I32)[None, :] * SORT_R + run_row_b - unit_start_b * ALIGN
    row = q * ALIGN + (oh_t * base[:, None, :]).sum(axis=2)
    m = jnp.arange(UNITS_PER_BLK, dtype=I32)[None, :]
    valid = m < n_valid[:, None]
    pad_rank = jnp.clip(m - n_valid[:, None], 0, UNITS_PER_BLK - 2)
    dump = N_TILES * SORT_R + (e_b[:, None] * (UNITS_PER_BLK - 1) + pad_rank) * ALIGN
    src = jnp.where(valid, row, 0).astype(I32).reshape(-1)
    dst = jnp.where(valid, row, dump).astype(I32).reshape(-1)
    return e_b, n_used.astype(I32), src, dst


def _expert_kernel(be_ref, nu_ref, src_ref, dst_ref,
                   xs_hbm, wg_ref, wu_ref, wd_ref, ys_in_hbm, ys_hbm,
                   xbuf, ybuf, wg_bf, wu_bf, wd_bf, sem_in, sem_out):
    del ys_in_hbm
    b = pl.program_id(0)
    n_used = nu_ref[0]
    slot = b % 2

    def start_in(blk, sl):
        rows = [pl.multiple_of(src_ref[blk * UNITS_PER_BLK + m], ALIGN) for m in range(UNITS_PER_BLK)]
        for m in range(UNITS_PER_BLK):
            pltpu.make_async_copy(xs_hbm.at[pl.ds(rows[m], ALIGN), :],
                                  xbuf.at[sl, pl.ds(m * ALIGN, ALIGN), :], sem_in.at[sl]).start()

    def start_out(blk, sl):
        rows = [pl.multiple_of(dst_ref[blk * UNITS_PER_BLK + m], ALIGN) for m in range(UNITS_PER_BLK)]
        for m in range(UNITS_PER_BLK):
            pltpu.make_async_copy(ybuf.at[sl, pl.ds(m * ALIGN, ALIGN), :],
                                  ys_hbm.at[pl.ds(rows[m], ALIGN), :], sem_out.at[sl]).start()

    def wait_in(sl):
        pltpu.make_async_copy(xs_hbm.at[pl.ds(0, EXP_BLK), :], xbuf.at[sl], sem_in.at[sl]).wait()

    def wait_out(sl):
        pltpu.make_async_copy(ybuf.at[sl], ys_hbm.at[pl.ds(0, EXP_BLK), :], sem_out.at[sl]).wait()

    @pl.when(b == 0)
    def _():
        start_in(0, 0)

    @pl.when(b < n_used)
    def _():
        wait_in(slot)

        @pl.when(b >= 2)
        def _():
            wait_out(slot)

        @pl.when((b == 0) | (be_ref[b] != be_ref[jnp.maximum(b - 1, 0)]))
        def _():
            wg_bf[...] = wg_ref[0, 0].astype(BF16)
            wu_bf[...] = wu_ref[0, 0].astype(BF16)
            wd_bf[...] = wd_ref[0, 0].astype(BF16)

        start_in(b + 1, 1 - slot)
        xb = xbuf[slot]
        a = _dot(xb, wg_bf[...])
        u = _dot(xb, wu_bf[...])
        ybuf[slot] = _dot((_silu(a) * u).astype(BF16), wd_bf[...]).astype(BF16)
        start_out(b, slot)

        @pl.when(b == n_used - 1)
        def _():
            wait_in(1 - slot)

            @pl.when(b >= 1)
            def _():
                wait_out(1 - slot)
            wait_out(slot)


def _expert_call(layer, block_e, n_used, src, dst, xs, wg, wu, wd, ys):
    def w_map(i, be, nu, sr, ds):
        return (layer, be[i], 0, 0)

    grid_spec = pltpu.PrefetchScalarGridSpec(
        num_scalar_prefetch=4,
        grid=(N_BLOCKS,),
        in_specs=[
            pl.BlockSpec(memory_space=pl.ANY),
            pl.BlockSpec((1, 1, D_MODEL, D_EXPERT), w_map),
            pl.BlockSpec((1, 1, D_MODEL, D_EXPERT), w_map),
            pl.BlockSpec((1, 1, D_EXPERT, D_MODEL), w_map),
            pl.BlockSpec(memory_space=pl.ANY),
        ],
        out_specs=pl.BlockSpec(memory_space=pl.ANY),
        scratch_shapes=[
            pltpu.VMEM((2, EXP_BLK, D_MODEL), BF16),
            pltpu.VMEM((2, EXP_BLK, D_MODEL), BF16),
            pltpu.VMEM((D_MODEL, D_EXPERT), BF16),
            pltpu.VMEM((D_MODEL, D_EXPERT), BF16),
            pltpu.VMEM((D_EXPERT, D_MODEL), BF16),
            pltpu.SemaphoreType.DMA((2,)),
            pltpu.SemaphoreType.DMA((2,)),
        ],
    )
    return pl.pallas_call(
        _expert_kernel,
        grid_spec=grid_spec,
        out_shape=jax.ShapeDtypeStruct((YS_ROWS, D_MODEL), BF16),
        input_output_aliases={8: 0},
        compiler_params=pltpu.CompilerParams(
            dimension_semantics=("arbitrary",), vmem_limit_bytes=VMEM_LIMIT, has_side_effects=True),
        name="moe_experts",
    )(block_e, n_used, src, dst, xs, wg, wu, wd, ys)


def kernel(x, c, w_ada, b_ada, norm1_g, norm2_g, w_in, w_ret_o, pool_w, pool_scale, w_pool_o, w_out,
           w_router, b_router, w_e_gate, w_e_up, w_e_down, final_g):
    assert x.shape == (BATCH, SEQ, D_MODEL) and w_in.shape == (DEPTH, D_MODEL, IN_W)
    xt = x.reshape(N_TOK, D_MODEL)
    mod = _ada_call(c, w_ada, b_ada).reshape(DEPTH, BATCH, 6, D_MODEL)

    rot = _rotary_tables()
    ret = _retention_tables()
    tri = (jnp.arange(TILE)[:, None] < jnp.arange(TILE)[None, :]).astype(BF16)
    wrt = w_router.T.astype(BF16)
    br = b_router.astype(F32).reshape(N_EXPERTS, 1)
    ys = jnp.zeros((YS_ROWS, D_MODEL), BF16)

    moe = None
    for l in range(DEPTH):
        g1 = norm1_g[l].reshape(1, D_MODEL)
        if moe is None:
            proj = _proj_call(xt, mod[l], g1, w_in[l].astype(BF16))
        else:
            xt, proj = _proj_call(xt, mod[l], g1, w_in[l].astype(BF16), moe)
        xt, xs, lw, n_pad = _mix_call(
            l, proj, xt, mod[l], norm2_g[l].reshape(1, D_MODEL), rot, ret,
            w_ret_o, pool_w, pool_scale[l].reshape(1, POOL_W), w_pool_o, w_out, wrt, br, tri)
        block_e, n_used, src, dst = _plan(n_pad[:, :, 0])
        ys = _expert_call(l, block_e, n_used, src, dst, xs.reshape(N_TILES * SORT_R, D_MODEL),
                          w_e_gate, w_e_up, w_e_down, ys)
        moe = (lw.T, ys, mod[l])
    out = _final_call(xt, *moe, final_g.reshape(1, D_MODEL))
    return out.reshape(BATCH, SEQ, D_MODEL)
```

```python
import functools

import numpy as np
import jax
import jax.numpy as jnp
from jax import lax
from jax.experimental import pallas as pl
from jax.experimental.pallas import tpu as pltpu

D_MODEL = 1024
BATCH = 16
SEQ = 2048
DEPTH = 4
N_TOK = BATCH * SEQ

RET_HEADS = 4
RET_QK_DIM = 128
RET_V_DIM = 256
RET_QK_W = RET_HEADS * RET_QK_DIM
RET_V_W = RET_HEADS * RET_V_DIM
ROPE_BASE = 10000.0
POOL_WINDOWS = (2, 4, 8, 16)
POOL_GROUPS = 4
POOL_GROUP_DIM = 128
POOL_W = POOL_GROUPS * POOL_GROUP_DIM
POOL_HALO = 16
IN_W = 2 * RET_QK_W + 2 * RET_V_W + POOL_W + 2 * D_MODEL
OFF_Q = 0
OFF_K = OFF_Q + RET_QK_W
OFF_V = OFF_K + RET_QK_W
OFF_G = OFF_V + RET_V_W
OFF_U = OFF_G + RET_V_W
OFF_AR = OFF_U + POOL_W
OFF_AP = OFF_AR + D_MODEL

N_EXPERTS = 16
N_GROUPS = 4
EXPERTS_PER_GROUP = 4
TOP_K = 2
D_EXPERT = D_MODEL // 2
EPS = 1e-6

TILE = 512
PROJ_TN = 512
RET_C = 256
N_TILES = N_TOK // TILE
ALIGN = 16
SORT_R = -(-(TOP_K * TILE + N_EXPERTS * (ALIGN - 1)) // 128) * 128
SORT_TN = 256
EXP_BLK = 1024
UNITS_PER_BLK = EXP_BLK // ALIGN
MAX_UNITS = N_TILES * ((TOP_K * TILE + N_EXPERTS * (ALIGN - 1)) // ALIGN)
N_BLOCKS = -(-MAX_UNITS // UNITS_PER_BLK) + N_EXPERTS
DUMP_UNITS = N_EXPERTS * (UNITS_PER_BLK - 1)
YS_ROWS = (N_TILES + -(-DUMP_UNITS * ALIGN // SORT_R)) * SORT_R
VMEM_LIMIT = 56 * 1024 * 1024

F32 = jnp.float32
BF16 = jnp.bfloat16
I32 = jnp.int32


def _const_spec(shape):
    zeros = (0,) * len(shape)
    return pl.BlockSpec(shape, lambda *_: zeros, pipeline_mode=pl.Buffered(1))


def _rms(xf):
    return xf * lax.rsqrt(jnp.mean(xf * xf, axis=-1, keepdims=True) + EPS)


def _dot(a, b):
    return jnp.dot(a, b, preferred_element_type=F32)


def _sigmoid(x):
    return 0.5 * jnp.tanh(0.5 * x) + 0.5


def _silu(x):
    return x * _sigmoid(x)


def _ada_kernel(c_ref, w_ref, b_ref, o_ref):
    cond = jax.nn.silu(c_ref[...]).astype(BF16)
    o_ref[0] = _dot(cond, w_ref[0].astype(BF16)) + b_ref[0]


def _ada_call(c, w_ada, b_ada):
    n_col = 6
    return pl.pallas_call(
        _ada_kernel,
        grid=(DEPTH, n_col),
        in_specs=[
            pl.BlockSpec((BATCH, D_MODEL), lambda l, j: (0, 0)),
            pl.BlockSpec((1, D_MODEL, D_MODEL), lambda l, j: (l, 0, j)),
            pl.BlockSpec((1, 1, D_MODEL), lambda l, j: (l, 0, j)),
        ],
        out_specs=pl.BlockSpec((1, BATCH, D_MODEL), lambda l, j: (l, 0, j)),
        out_shape=jax.ShapeDtypeStruct((DEPTH, BATCH, n_col * D_MODEL), F32),
        compiler_params=pltpu.CompilerParams(
            dimension_semantics=("arbitrary", "arbitrary"), vmem_limit_bytes=VMEM_LIMIT),
        name="ada_mod",
    )(c, w_ada, b_ada.reshape(DEPTH, 1, n_col * D_MODEL))


def _combine_weights(cols):
    lane = lax.broadcasted_iota(I32, (TILE, SORT_R), 1)
    l0 = cols[:, 0:1].astype(I32)
    l1 = cols[:, 1:2].astype(I32)
    pw = jnp.where(lane == l0, cols[:, 2:3], 0.0) + jnp.where(lane == l1, cols[:, 3:4], 0.0)
    return pw.astype(BF16)


def _moe_combine(x, cols, ys, gate):
    return x + gate * _dot(_combine_weights(cols), ys)


def _proj_kernel(combine, *refs):
    if combine:
        x_ref, cols_ref, ys_ref, modp_ref, mod_ref, g_ref, w_ref, xo_ref, o_ref, h_prev = refs
    else:
        x_ref, mod_ref, g_ref, w_ref, o_ref, h_prev = refs

    @pl.when(pl.program_id(0) == 0)
    def _():
        h_prev[...] = jnp.zeros_like(h_prev)

    hp = h_prev[...]
    n_chunks = IN_W // PROJ_TN

    def proj_chunks(lo, hi):
        for n in range(lo, hi):
            cols = slice(n * PROJ_TN, (n + 1) * PROJ_TN)
            o_ref[:, cols] = _dot(hp, w_ref[:, cols]).astype(BF16)

    if combine:
        pw = _combine_weights(cols_ref[...])
        proj_chunks(0, n_chunks // 2)
        x = x_ref[...] + modp_ref[0, 5:6, :] * _dot(pw, ys_ref[...])
        xo_ref[...] = x
        proj_chunks(n_chunks // 2, n_chunks)
    else:
        x = x_ref[...]
        proj_chunks(0, n_chunks)
    h = (_rms(x) * g_ref[...]) * (1.0 + mod_ref[0, 1:2, :]) + mod_ref[0, 0:1, :]
    h_prev[...] = h.astype(BF16)


def _proj_call(x, mod_l, g, w_in_l, moe=None):
    tiles_per_b = SEQ // TILE
    last = N_TILES - 1
    cur = lambda i: jnp.minimum(i, last)
    tok_in = pl.BlockSpec((TILE, D_MODEL), lambda i: (cur(i), 0))
    mod_spec = pl.BlockSpec((1, 6, D_MODEL), lambda i: (cur(i) // tiles_per_b, 0, 0))
    proj_spec = pl.BlockSpec((TILE, IN_W), lambda i: (jnp.maximum(i - 1, 0), 0))
    proj_shape = jax.ShapeDtypeStruct((N_TOK, IN_W), BF16)
    tail_specs = [mod_spec, _const_spec((1, D_MODEL)), _const_spec((D_MODEL, IN_W))]
    if moe is None:
        in_specs = [tok_in] + tail_specs
        args = (x, mod_l, g, w_in_l)
        out_specs, out_shape = proj_spec, proj_shape
    else:
        cols, ys, mod_prev = moe
        in_specs = [tok_in, pl.BlockSpec((TILE, 4), lambda i: (cur(i), 0)),
                    pl.BlockSpec((SORT_R, D_MODEL), lambda i: (cur(i), 0)), mod_spec] + tail_specs
        args = (x, cols, ys, mod_prev, mod_l, g, w_in_l)
        out_specs = [pl.BlockSpec((TILE, D_MODEL), lambda i: (i, 0)), proj_spec]
        out_shape = [jax.ShapeDtypeStruct((N_TOK + TILE, D_MODEL), F32), proj_shape]
    return pl.pallas_call(
        functools.partial(_proj_kernel, moe is not None),
        grid=(N_TILES + 1,),
        in_specs=in_specs,
        out_specs=out_specs,
        out_shape=out_shape,
        scratch_shapes=[pltpu.VMEM((TILE, D_MODEL), BF16)],
        compiler_params=pltpu.CompilerParams(
            dimension_semantics=("arbitrary",), vmem_limit_bytes=VMEM_LIMIT),
        name="in_proj",
    )(*args)


def _final_kernel(x_ref, cols_ref, ys_ref, modp_ref, g_ref, o_ref):
    x = _moe_combine(x_ref[...], cols_ref[...], ys_ref[...], modp_ref[0, 5:6, :])
    o_ref[...] = _rms(x) * g_ref[...]


def _final_call(x, cols, ys, mod_prev, final_g):
    tiles_per_b = SEQ // TILE
    tok = pl.BlockSpec((TILE, D_MODEL), lambda i: (i, 0))
    return pl.pallas_call(
        _final_kernel,
        grid=(N_TILES,),
        in_specs=[tok, pl.BlockSpec((TILE, 4), lambda i: (i, 0)),
                  pl.BlockSpec((SORT_R, D_MODEL), lambda i: (i, 0)),
                  pl.BlockSpec((1, 6, D_MODEL), lambda i: (i // tiles_per_b, 0, 0)),
                  _const_spec((1, D_MODEL))],
        out_specs=tok,
        out_shape=jax.ShapeDtypeStruct((N_TOK, D_MODEL), F32),
        compiler_params=pltpu.CompilerParams(
            dimension_semantics=("arbitrary",), vmem_limit_bytes=VMEM_LIMIT),
        name="final_norm",
    )(x, cols, ys, mod_prev, final_g)


def _top2_sum(a, b, c, d):
    hi1, lo1 = jnp.maximum(a, b), jnp.minimum(a, b)
    hi2, lo2 = jnp.maximum(c, d), jnp.minimum(c, d)
    return jnp.maximum(hi1, hi2) + jnp.maximum(jnp.minimum(hi1, hi2), jnp.maximum(lo1, lo2))


def _first_argmax(vals):
    best = vals[0]
    idx = jnp.zeros(best.shape, I32)
    for j in range(1, len(vals)):
        better = vals[j] > best
        idx = jnp.where(better, j, idx)
        best = jnp.where(better, vals[j], best)
    return idx, best


def _mix_kernel(gamma_c,
                proj_ref, x_ref, mod_ref, n2g_ref, cq_ref, sq_ref, ck_ref, sk_ref,
                dec_ref, xi_ref, zeta_ref, wro_ref, pw_ref, ps_ref, wpo_ref, wout_ref,
                wrt_ref, br_ref, tri_ref,
                xo_ref, xs_ref, lw_ref, np_ref,
                state_ref, halo_ref, gated_ref, wro_bf, pw_bf, wpo_bf, wout_bf):
    s = pl.program_id(1)
    ts = TILE

    @pl.when((pl.program_id(0) == 0) & (s == 0))
    def _():
        wro_bf[...] = wro_ref[0].astype(BF16)
        pw_bf[...] = pw_ref[0].astype(BF16)
        wpo_bf[...] = wpo_ref[0].astype(BF16)
        wout_bf[...] = wout_ref[0].astype(BF16)

    @pl.when(s == 0)
    def _():
        state_ref[...] = jnp.zeros_like(state_ref)
        halo_ref[...] = jnp.zeros_like(halo_ref)

    for j in range(ts // RET_C):
        rows = slice(j * RET_C, (j + 1) * RET_C)
        cq, sq = cq_ref[rows, :], sq_ref[rows, :]
        ck, sk = ck_ref[rows, :], sk_ref[rows, :]
        for h in range(RET_HEADS):
            qh = proj_ref[rows, OFF_Q + h * RET_QK_DIM:OFF_Q + (h + 1) * RET_QK_DIM].astype(F32)
            kh = proj_ref[rows, OFF_K + h * RET_QK_DIM:OFF_K + (h + 1) * RET_QK_DIM].astype(F32)
            vh = proj_ref[rows, OFF_V + h * RET_V_DIM:OFF_V + (h + 1) * RET_V_DIM]
            qh = qh * cq + pltpu.roll(qh, RET_QK_DIM // 2, axis=1) * sq
            kh = kh * ck + pltpu.roll(kh, RET_QK_DIM // 2, axis=1) * sk
            sc = lax.dot_general(qh.astype(BF16), kh.astype(BF16), (((1,), (1,)), ((), ())),
                                 preferred_element_type=F32) * dec_ref[h]
            st = state_ref[h]
            y = _dot(sc.astype(BF16), vh) + _dot((qh * xi_ref[h]).astype(BF16), st.astype(BF16))
            kv = lax.dot_general((kh * zeta_ref[h]).astype(BF16), vh, (((0,), (0,)), ((), ())),
                                 preferred_element_type=F32)
            state_ref[h] = st * gamma_c[h] + kv
            gh = proj_ref[rows, OFF_G + h * RET_V_DIM:OFF_G + (h + 1) * RET_V_DIM].astype(F32)
            gated_ref[rows, h * RET_V_DIM:(h + 1) * RET_V_DIM] = (_silu(gh) * _rms(y)).astype(BF16)
    y_ret = _dot(gated_ref[...], wro_bf[...])

    u = proj_ref[:, OFF_U:OFF_U + POOL_W].astype(F32)
    ext = jnp.concatenate([halo_ref[...], u], axis=0)
    halo_ref[...] = u[ts - POOL_HALO:, :]
    pos = s * ts + lax.broadcasted_iota(I32, (ts, 1), 0)
    pooled = []
    for g, win in enumerate(POOL_WINDOWS):
        cols = slice(g * POOL_GROUP_DIM, (g + 1) * POOL_GROUP_DIM)
        acc = ext[:, cols]
        shift = 1
        while shift < win:
            acc = acc + pltpu.roll(acc, shift, axis=0)
            shift *= 2
        count = jnp.minimum(pos + 1, win).astype(F32)
        mix = acc[POOL_HALO:, :] / count - u[:, cols]
        pooled.append(_dot(mix.astype(BF16), pw_bf[g]))
    y_pool = _dot((jnp.concatenate(pooled, axis=1) * ps_ref[...]).astype(BF16), wpo_bf[...])

    a_ret = proj_ref[:, OFF_AR:OFF_AR + D_MODEL].astype(F32)
    a_pool = proj_ref[:, OFF_AP:OFF_AP + D_MODEL].astype(F32)
    merged = _sigmoid(a_ret) * y_ret + _sigmoid(a_pool) * y_pool
    x_new = x_ref[...] + mod_ref[0, 2:3, :] * _dot(merged.astype(BF16), wout_bf[...])
    xo_ref[...] = x_new

    h2 = (_rms(x_new) * n2g_ref[...]) * (1.0 + mod_ref[0, 4:5, :]) + mod_ref[0, 3:4, :]
    h2b = h2.astype(BF16)

    logits = lax.dot_general(wrt_ref[...], h2b, (((1,), (1,)), ((), ())),
                             preferred_element_type=F32) + br_ref[...]
    p = jnp.exp(logits - jnp.max(logits, axis=0, keepdims=True))
    probs = p / jnp.sum(p, axis=0, keepdims=True)
    pr = [probs[e:e + 1, :] for e in range(N_EXPERTS)]
    scores = [_top2_sum(*pr[g * EXPERTS_PER_GROUP:(g + 1) * EXPERTS_PER_GROUP]) for g in range(N_GROUPS)]
    grp, _ = _first_argmax(scores)
    in_grp = []
    for jj in range(EXPERTS_PER_GROUP):
        v = pr[jj]
        for g in range(1, N_GROUPS):
            v = jnp.where(grp == g, pr[g * EXPERTS_PER_GROUP + jj], v)
        in_grp.append(v)
    i0, p0 = _first_argmax(in_grp)
    i1, p1 = _first_argmax([jnp.where(i0 == jj, -1.0, in_grp[jj]) for jj in range(EXPERTS_PER_GROUP)])
    e0 = grp * EXPERTS_PER_GROUP + i0
    e1 = grp * EXPERTS_PER_GROUP + i1
    den = p0 + p1

    erow = lax.broadcasted_iota(I32, (N_EXPERTS, ts), 0)
    oh0 = erow == e0
    oh1 = erow == e1
    ohf = jnp.where(oh0 | oh1, 1.0, 0.0)
    n_e = jnp.sum(ohf, axis=1, keepdims=True)
    n_pad = jnp.broadcast_to(jnp.floor((n_e + (ALIGN - 1.0)) * (1.0 / ALIGN)) * ALIGN, (N_EXPERTS, 128))
    np_ref[0] = n_pad
    erow128 = lax.broadcasted_iota(I32, (N_EXPERTS, 128), 0)
    run_end = n_pad
    shift = 1
    while shift < N_EXPERTS:
        run_end = run_end + jnp.where(erow128 >= shift, pltpu.roll(run_end, shift, axis=0), 0.0)
        shift *= 2
    run_start = (run_end - n_pad)[:, 0:1]
    row_of = _dot(ohf.astype(BF16), tri_ref[...]) + run_start
    l0 = jnp.sum(jnp.where(oh0, row_of, 0.0), axis=0, keepdims=True)
    l1 = jnp.sum(jnp.where(oh1, row_of, 0.0), axis=0, keepdims=True)
    lw_ref[...] = jnp.concatenate([l0, l1, p0 / den, p1 / den], axis=0)

    srow = lax.broadcasted_iota(I32, (SORT_R, ts), 0)
    perm = jnp.where((srow == l0.astype(I32)) | (srow == l1.astype(I32)), 1.0, 0.0).astype(BF16)
    for n in range(D_MODEL // SORT_TN):
        cols = slice(n * SORT_TN, (n + 1) * SORT_TN)
        xs_ref[0, :, cols] = _dot(perm, h2b[:, cols]).astype(BF16)


def _retention_tables():
    lg = np.log(1.0 - 2.0 ** (-5.0 - np.arange(RET_HEADS, dtype=np.float32))).astype(np.float32)
    idx = np.arange(RET_C, dtype=np.float32)
    rel = idx[:, None] - idx[None, :]
    causal = rel >= 0
    dec = np.where(causal[None], np.exp(lg[:, None, None] * np.where(causal, rel, 0.0)[None]), 0.0)
    xi = np.exp(lg[:, None] * (idx[None, :] + 1.0))
    zeta = np.exp(lg[:, None] * (RET_C - 1.0 - idx[None, :]))
    gamma_c = tuple(float(v) for v in np.exp(lg * RET_C).astype(np.float32))
    xi = np.broadcast_to(xi[:, :, None], (RET_HEADS, RET_C, RET_QK_DIM))
    zeta = np.broadcast_to(zeta[:, :, None], (RET_HEADS, RET_C, RET_QK_DIM))
    return (jnp.asarray(dec, F32), jnp.asarray(xi, F32), jnp.asarray(zeta, F32), gamma_c)


def _rotary_tables():
    freqs = ROPE_BASE ** (-jnp.arange(0, RET_QK_DIM, 2, dtype=F32) / RET_QK_DIM)
    ang = jnp.arange(SEQ, dtype=F32)[:, None] * freqs[None, :]
    cos, sin = jnp.cos(ang), jnp.sin(ang)
    cos2 = jnp.concatenate([cos, cos], axis=1)
    sin2 = jnp.concatenate([-sin, sin], axis=1)
    k_scale = RET_QK_DIM ** -0.5
    return cos2, sin2, cos2 * k_scale, sin2 * k_scale


def _mix_call(layer, proj, x, mod_l, n2g, rot, ret, wro, pw, ps, wpo, wout, wrt, br, tri):
    dec, xi, zeta, gamma_c = ret

    def layer_spec(shape):
        zeros = (0,) * len(shape)
        return pl.BlockSpec((1,) + shape, lambda b, s: (layer,) + zeros, pipeline_mode=pl.Buffered(1))

    ts = TILE
    n_s = SEQ // ts
    tok = lambda b, s: (b * n_s + s, 0)
    seq_tab = pl.BlockSpec((ts, RET_QK_DIM), lambda b, s: (s, 0))
    return pl.pallas_call(
        functools.partial(_mix_kernel, gamma_c),
        grid=(BATCH, n_s),
        in_specs=[
            pl.BlockSpec((ts, IN_W), tok),
            pl.BlockSpec((ts, D_MODEL), tok),
            pl.BlockSpec((1, 6, D_MODEL), lambda b, s: (b, 0, 0)),
            _const_spec((1, D_MODEL)),
            seq_tab, seq_tab, seq_tab, seq_tab,
            _const_spec((RET_HEADS, RET_C, RET_C)),
            _const_spec((RET_HEADS, RET_C, RET_QK_DIM)),
            _const_spec((RET_HEADS, RET_C, RET_QK_DIM)),
            layer_spec((RET_V_W, D_MODEL)),
            layer_spec((POOL_GROUPS, POOL_GROUP_DIM, POOL_GROUP_DIM)),
            _const_spec((1, POOL_W)),
            layer_spec((POOL_W, D_MODEL)),
            layer_spec((D_MODEL, D_MODEL)),
            _const_spec((N_EXPERTS, D_MODEL)),
            _const_spec((N_EXPERTS, 1)),
            _const_spec((ts, ts)),
        ],
        out_specs=[
            pl.BlockSpec((ts, D_MODEL), tok),
            pl.BlockSpec((1, SORT_R, D_MODEL), lambda b, s: (b * n_s + s, 0, 0)),
            pl.BlockSpec((4, ts), lambda b, s: (0, b * n_s + s)),
            pl.BlockSpec((1, N_EXPERTS, 128), lambda b, s: (b * n_s + s, 0, 0)),
        ],
        out_shape=[
            jax.ShapeDtypeStruct((N_TOK, D_MODEL), F32),
            jax.ShapeDtypeStruct((N_TILES, SORT_R, D_MODEL), BF16),
            jax.ShapeDtypeStruct((4, N_TOK), F32),
            jax.ShapeDtypeStruct((N_TILES, N_EXPERTS, 128), F32),
        ],
        scratch_shapes=[
            pltpu.VMEM((RET_HEADS, RET_QK_DIM, RET_V_DIM), F32),
            pltpu.VMEM((POOL_HALO, POOL_W), F32),
            pltpu.VMEM((ts, RET_V_W), BF16),
            pltpu.VMEM((RET_V_W, D_MODEL), BF16),
            pltpu.VMEM((POOL_GROUPS, POOL_GROUP_DIM, POOL_GROUP_DIM), BF16),
            pltpu.VMEM((POOL_W, D_MODEL), BF16),
            pltpu.VMEM((D_MODEL, D_MODEL), BF16),
        ],
        compiler_params=pltpu.CompilerParams(
            dimension_semantics=("arbitrary", "arbitrary"), vmem_limit_bytes=VMEM_LIMIT),
        name="mixer_route",
    )(proj, x, mod_l, n2g, *rot, dec, xi, zeta, wro, pw, ps, wpo, wout, wrt, br, tri)


def _plan(n_pad):
    n_pad = n_pad.astype(I32)
    units = (n_pad // ALIGN).T
    run_row = (jnp.cumsum(n_pad, axis=1) - n_pad).T
    unit_start = jnp.cumsum(units, axis=1) - units
    total = units.sum(axis=1)
    n_blk = (total + UNITS_PER_BLK - 1) // UNITS_PER_BLK
    blk_end = jnp.cumsum(n_blk)
    blk_start = blk_end - n_blk
    n_used = blk_end[-1:]
    b = jnp.arange(N_BLOCKS + 1, dtype=I32)
    e_b = jnp.minimum((b[:, None] >= blk_end[None, :]).sum(axis=1), N_EXPERTS - 1).astype(I32)
    oh_e = (e_b[:, None] == jnp.arange(N_EXPERTS, dtype=I32)[None, :]).astype(I32)
    pick = lambda tab: (oh_e[:, :, None] * tab[None, :, :]).sum(axis=1)
    unit_start_b, units_b, run_row_b = pick(unit_start), pick(units), pick(run_row)
    total_b = (oh_e * total[None, :]).sum(axis=1)
    first_q = (b - (oh_e * blk_start[None, :]).sum(axis=1)) * UNITS_PER_BLK
    n_valid = jnp.where(b < n_used[0], jnp.clip(total_b - first_q, 0, UNITS_PER_BLK), 0).astype(I32)
    q = first_q[:, None] + jnp.arange(UNITS_PER_BLK, dtype=I32)[None, :]
    tile = jnp.minimum(((unit_start_b + units_b)[:, None, :] <= q[:, :, None]).sum(axis=2), N_TILES - 1)
    oh_t = (tile[:, :, None] == jnp.arange(N_TILES, dtype=I32)[None, None, :]).astype(I32)
    base = jnp.arange(N_TILES, dtype=I32)[None, :] * SORT_R + run_row_b - unit_start_b * ALIGN
    row = q * ALIGN + (oh_t * base[:, None, :]).sum(axis=2)
    m = jnp.arange(UNITS_PER_BLK, dtype=I32)[None, :]
    valid = m < n_valid[:, None]
    pad_rank = jnp.clip(m - n_valid[:, None], 0, UNITS_PER_BLK - 2)
    dump = N_TILES * SORT_R + (e_b[:, None] * (UNITS_PER_BLK - 1) + pad_rank) * ALIGN
    src = jnp.where(valid, row, 0).astype(I32).reshape(-1)
    dst = jnp.where(valid, row, dump).astype(I32).reshape(-1)
    return e_b, n_used.astype(I32), src, dst


def _expert_kernel(be_ref, nu_ref, src_ref, dst_ref,
                   xs_hbm, wg_ref, wu_ref, wd_ref, ys_in_hbm, ys_hbm,
                   xbuf, ybuf, wg_bf, wu_bf, wd_bf, sem_in, sem_out):
    del ys_in_hbm
    b = pl.program_id(0)
    n_used = nu_ref[0]
    slot = b % 2

    def start_in(blk, sl):
        rows = [pl.multiple_of(src_ref[blk * UNITS_PER_BLK + m], ALIGN) for m in range(UNITS_PER_BLK)]
        for m in range(UNITS_PER_BLK):
            pltpu.make_async_copy(xs_hbm.at[pl.ds(rows[m], ALIGN), :],
                                  xbuf.at[sl, pl.ds(m * ALIGN, ALIGN), :], sem_in.at[sl]).start(priority=m % 2)

    def start_out(blk, sl):
        rows = [pl.multiple_of(dst_ref[blk * UNITS_PER_BLK + m], ALIGN) for m in range(UNITS_PER_BLK)]
        for m in range(UNITS_PER_BLK):
            pltpu.make_async_copy(ybuf.at[sl, pl.ds(m * ALIGN, ALIGN), :],
                                  ys_hbm.at[pl.ds(rows[m], ALIGN), :], sem_out.at[sl]).start(priority=m % 2)

    def wait_in(sl):
        pltpu.make_async_copy(xs_hbm.at[pl.ds(0, EXP_BLK), :], xbuf.at[sl], sem_in.at[sl]).wait()

    def wait_out(sl):
        pltpu.make_async_copy(ybuf.at[sl], ys_hbm.at[pl.ds(0, EXP_BLK), :], sem_out.at[sl]).wait()

    @pl.when(b == 0)
    def _():
        start_in(0, 0)

    @pl.when(b < n_used)
    def _():
        wait_in(slot)

        @pl.when(b >= 2)
        def _():
            wait_out(slot)

        @pl.when((b == 0) | (be_ref[b] != be_ref[jnp.maximum(b - 1, 0)]))
        def _():
            wg_bf[...] = wg_ref[0, 0].astype(BF16)
            wu_bf[...] = wu_ref[0, 0].astype(BF16)
            wd_bf[...] = wd_ref[0, 0].astype(BF16)

        start_in(b + 1, 1 - slot)
        xb = xbuf[slot]
        a = _dot(xb, wg_bf[...])
        u = _dot(xb, wu_bf[...])
        ybuf[slot] = _dot((_silu(a) * u).astype(BF16), wd_bf[...]).astype(BF16)
        start_out(b, slot)

        @pl.when(b == n_used - 1)
        def _():
            wait_in(1 - slot)

            @pl.when(b >= 1)
            def _():
                wait_out(1 - slot)
            wait_out(slot)


def _expert_call(layer, block_e, n_used, src, dst, xs, wg, wu, wd, ys):
    def w_map(i, be, nu, sr, ds):
        return (layer, be[i], 0, 0)

    grid_spec = pltpu.PrefetchScalarGridSpec(
        num_scalar_prefetch=4,
        grid=(N_BLOCKS,),
        in_specs=[
            pl.BlockSpec(memory_space=pl.ANY),
            pl.BlockSpec((1, 1, D_MODEL, D_EXPERT), w_map),
            pl.BlockSpec((1, 1, D_MODEL, D_EXPERT), w_map),
            pl.BlockSpec((1, 1, D_EXPERT, D_MODEL), w_map),
            pl.BlockSpec(memory_space=pl.ANY),
        ],
        out_specs=pl.BlockSpec(memory_space=pl.ANY),
        scratch_shapes=[
            pltpu.VMEM((2, EXP_BLK, D_MODEL), BF16),
            pltpu.VMEM((2, EXP_BLK, D_MODEL), BF16),
            pltpu.VMEM((D_MODEL, D_EXPERT), BF16),
            pltpu.VMEM((D_MODEL, D_EXPERT), BF16),
            pltpu.VMEM((D_EXPERT, D_MODEL), BF16),
            pltpu.SemaphoreType.DMA((2,)),
            pltpu.SemaphoreType.DMA((2,)),
        ],
    )
    return pl.pallas_call(
        _expert_kernel,
        grid_spec=grid_spec,
        out_shape=jax.ShapeDtypeStruct((YS_ROWS, D_MODEL), BF16),
        input_output_aliases={8: 0},
        compiler_params=pltpu.CompilerParams(
            dimension_semantics=("arbitrary",), vmem_limit_bytes=VMEM_LIMIT, has_side_effects=True),
        name="moe_experts",
    )(block_e, n_used, src, dst, xs, wg, wu, wd, ys)


def kernel(x, c, w_ada, b_ada, norm1_g, norm2_g, w_in, w_ret_o, pool_w, pool_scale, w_pool_o, w_out,
           w_router, b_router, w_e_gate, w_e_up, w_e_down, final_g):
    assert x.shape == (BATCH, SEQ, D_MODEL) and w_in.shape == (DEPTH, D_MODEL, IN_W)
    xt = x.reshape(N_TOK, D_MODEL)
    mod = _ada_call(c, w_ada, b_ada).reshape(DEPTH, BATCH, 6, D_MODEL)

    rot = _rotary_tables()
    ret = _retention_tables()
    tri = (jnp.arange(TILE)[:, None] < jnp.arange(TILE)[None, :]).astype(BF16)
    wrt = w_router.T.astype(BF16)
    br = b_router.astype(F32).reshape(N_EXPERTS, 1)
    ys = jnp.zeros((YS_ROWS, D_MODEL), BF16)

    moe = None
    for l in range(DEPTH):
        g1 = norm1_g[l].reshape(1, D_MODEL)
        if moe is None:
            proj = _proj_call(xt, mod[l], g1, w_in[l].astype(BF16))
        else:
            xt, proj = _proj_call(xt, mod[l], g1, w_in[l].astype(BF16), moe)
        xt, xs, lw, n_pad = _mix_call(
            l, proj, xt, mod[l], norm2_g[l].reshape(1, D_MODEL), rot, ret,
            w_ret_o, pool_w, pool_scale[l].reshape(1, POOL_W), w_pool_o, w_out, wrt, br, tri)
        block_e, n_used, src, dst = _plan(n_pad[:, :, 0])
        ys = _expert_call(l, block_e, n_used, src, dst, xs.reshape(N_TILES * SORT_R, D_MODEL),
                          w_e_gate, w_e_up, w_e_down, ys)
        moe = (lw.T, ys, mod[l])
    out = _final_call(xt, *moe, final_g.reshape(1, D_MODEL))
    return out.reshape(BATCH, SEQ, D_MODEL)
```
